```python
import jax, jax.numpy as jnp
from jax import lax
import numpy as np

D_MODEL = 2048
BATCH = 1
SEQ = 16384
DEPTH = 2
DEC_BATCH = 8
DEC_SEQ = 32
PAST_LEN = 4096

CHUNK = 64
N_META = 16
N_MIXERS = 2
N_A = (DEPTH + 1) // 2
N_B = DEPTH // 2
D_RNN = 2560
RG_BLOCK = 128
N_RG_BLOCKS = D_RNN // RG_BLOCK
CONV_A = 4
RG_C = 8.0
D_CONV = D_MODEL
CONV_B = 3
D_FF = 5632
EPS = 1e-6
N_NORMS = 6

kernel_name = "hawk_shortconv_macaron_stream_step"


def rms_norm(x, g):
    xf = x.astype(jnp.float32)
    y = xf * lax.rsqrt(jnp.mean(xf * xf, axis=-1, keepdims=True) + EPS)
    return (y * g.astype(jnp.float32)).astype(x.dtype)


def swiglu(x, wg, wu, wd):
    return (jax.nn.silu(x @ wg) * (x @ wu)) @ wd


def causal_dwconv(u, hist, w):
    width = w.shape[0]
    t = u.shape[1]
    full = jnp.concatenate([hist.astype(u.dtype), u], axis=1)
    out = full[:, 0:t] * w[0]
    for k in range(1, width):
        out = out + full[:, k:k + t] * w[k]
    return out, full[:, full.shape[1] - (width - 1):]


def _lin_combine(c1, c2):
    a1, b1 = c1
    a2, b2 = c2
    return a1 * a2, a2 * b1 + b2


def rglru_mixer(x, h0, conv_hist, w_in, conv_w, conv_b, gate_w, gate_b, lam, w_out):
    bsz, t, _ = x.shape
    proj = x @ w_in
    gate_branch, u = proj[..., :D_RNN], proj[..., D_RNN:]
    u, new_hist = causal_dwconv(u, conv_hist, conv_w)
    u = u + conv_b
    ub = u.reshape(bsz, t, N_RG_BLOCKS, RG_BLOCK)
    g = jnp.einsum('btnk,nkj->btnj', ub, gate_w) + gate_b
    g = g.astype(jnp.float32)
    r = jax.nn.sigmoid(g[..., :RG_BLOCK]).reshape(bsz, t, D_RNN)
    i = jax.nn.sigmoid(g[..., RG_BLOCK:]).reshape(bsz, t, D_RNN)
    log_a = -RG_C * r * jax.nn.softplus(-lam.astype(jnp.float32))
    a = jnp.exp(log_a)
    b = jnp.sqrt(-jnp.expm1(2.0 * log_a)) * (i * u.astype(jnp.float32))
    b = b.at[:, 0].add(a[:, 0] * h0.astype(jnp.float32))
    _, h = lax.associative_scan(_lin_combine, (a, b), axis=1)
    y = (jax.nn.gelu(gate_branch) * h.astype(x.dtype)) @ w_out
    return y, h[:, -1].astype(h0.dtype), new_hist


def shortconv_mixer(x, conv_hist, w_in, conv_w, w_out):
    proj = x @ w_in
    gb, gc, v = proj[..., :D_CONV], proj[..., D_CONV:2 * D_CONV], proj[..., 2 * D_CONV:]
    z, new_hist = causal_dwconv(gc * v, conv_hist, conv_w)
    return (gb * z) @ w_out, new_hist


def trunk(x, h_a, hist_a, hist_b, p):
    new_h, new_ha, new_hb = [], [], []
    for layer in range(DEPTH):
        g = p['norm_g'][layer]
        f1 = swiglu(rms_norm(x, g[0]), p['ffn1_wg'][layer], p['ffn1_wu'][layer], p['ffn1_wd'][layer])
        x = x + 0.5 * rms_norm(f1, g[1])
        j = layer // N_MIXERS
        xn = rms_norm(x, g[2])
        if layer % N_MIXERS == 0:
            m, hT, ha = rglru_mixer(xn, h_a[j], hist_a[j], p['a_w_in'][j], p['a_conv_w'][j], p['a_conv_b'][j],
                                    p['a_gate_w'][j], p['a_gate_b'][j], p['a_lambda'][j], p['a_w_out'][j])
            new_h.append(hT)
            new_ha.append(ha)
        else:
            m, hb = shortconv_mixer(xn, hist_b[j], p['b_w_in'][j], p['b_conv_w'][j], p['b_w_out'][j])
            new_hb.append(hb)
        x = x + rms_norm(m, g[3])
        f2 = swiglu(rms_norm(x, g[4]), p['ffn2_wg'][layer], p['ffn2_wu'][layer], p['ffn2_wd'][layer])
        x = x + 0.5 * rms_norm(f2, g[5])
    return x, jnp.stack(new_h), jnp.stack(new_ha), jnp.stack(new_hb)


def setup_inputs(seed: int = 0) -> dict:
    key = jax.random.key(seed)
    ks = jax.random.split(key, 24)
    f32 = jnp.float32

    def nrm(k, shape, scale):
        return jax.random.normal(k, shape, f32) * scale

    a0 = jax.random.uniform(ks[21], (N_A, D_RNN), f32, 0.9, 0.999)
    return {
        'x_prompt': nrm(ks[0], (BATCH, SEQ, D_MODEL), 1.0),
        'x_sample': nrm(ks[1], (DEC_BATCH, DEC_SEQ, D_MODEL), 1.0),
        'state_a_h': nrm(ks[2], (N_A, DEC_BATCH, D_RNN), 0.5),
        'cache_a_conv': nrm(ks[3], (N_A, DEC_BATCH, CONV_A - 1, D_RNN), 1.0),
        'cache_b_conv': nrm(ks[4], (N_B, DEC_BATCH, CONV_B - 1, D_CONV), 1.0),
        'meta_tokens': nrm(ks[5], (N_META, D_MODEL), 1.0),
        'norm_g': 1.0 + nrm(ks[6], (DEPTH, N_NORMS, D_MODEL), 0.02),
        'ffn1_wg': nrm(ks[7], (DEPTH, D_MODEL, D_FF), D_MODEL ** -0.5),
        'ffn1_wu': nrm(ks[8], (DEPTH, D_MODEL, D_FF), D_MODEL ** -0.5),
        'ffn1_wd': nrm(ks[9], (DEPTH, D_FF, D_MODEL), D_FF ** -0.5),
        'ffn2_wg': nrm(ks[10], (DEPTH, D_MODEL, D_FF), D_MODEL ** -0.5),
        'ffn2_wu': nrm(ks[11], (DEPTH, D_MODEL, D_FF), D_MODEL ** -0.5),
        'ffn2_wd': nrm(ks[12], (DEPTH, D_FF, D_MODEL), D_FF ** -0.5),
        'a_w_in': nrm(ks[13], (N_A, D_MODEL, 2 * D_RNN), D_MODEL ** -0.5),
        'a_conv_w': nrm(ks[14], (N_A, CONV_A, D_RNN), CONV_A ** -0.5),
        'a_conv_b': nrm(ks[15], (N_A, D_RNN), 0.01),
        'a_gate_w': nrm(ks[16], (N_A, N_RG_BLOCKS, RG_BLOCK, 2 * RG_BLOCK), RG_BLOCK ** -0.5),
        'a_gate_b': nrm(ks[17], (N_A, N_RG_BLOCKS, 2 * RG_BLOCK), 0.01),
        'a_lambda': jnp.log(a0 / (1.0 - a0)),
        'a_w_out': nrm(ks[18], (N_A, D_RNN, D_MODEL), D_RNN ** -0.5),
        'b_w_in': nrm(ks[19], (N_B, D_MODEL, 3 * D_CONV), D_MODEL ** -0.5),
        'b_conv_w': nrm(ks[20], (N_B, CONV_B, D_CONV), CONV_B ** -0.5),
        'b_w_out': nrm(ks[22], (N_B, D_CONV, D_MODEL), D_CONV ** -0.5),
    }


def reference(x_prompt, x_sample, state_a_h, cache_a_conv, cache_b_conv, meta_tokens, norm_g,
              ffn1_wg, ffn1_wu, ffn1_wd, ffn2_wg, ffn2_wu, ffn2_wd,
              a_w_in, a_conv_w, a_conv_b, a_gate_w, a_gate_b, a_lambda, a_w_out,
              b_w_in, b_conv_w, b_w_out):
    p = {'norm_g': norm_g,
         'ffn1_wg': ffn1_wg, 'ffn1_wu': ffn1_wu, 'ffn1_wd': ffn1_wd,
         'ffn2_wg': ffn2_wg, 'ffn2_wu': ffn2_wu, 'ffn2_wd': ffn2_wd,
         'a_w_in': a_w_in, 'a_conv_w': a_conv_w, 'a_conv_b': a_conv_b, 'a_gate_w': a_gate_w,
         'a_gate_b': a_gate_b, 'a_lambda': a_lambda, 'a_w_out': a_w_out,
         'b_w_in': b_w_in, 'b_conv_w': b_conv_w, 'b_w_out': b_w_out}

    bp = x_prompt.shape[0]
    meta = jnp.broadcast_to(meta_tokens.astype(x_prompt.dtype)[None], (bp, N_META, D_MODEL))
    xp = jnp.concatenate([meta, x_prompt], axis=1)
    h0 = jnp.zeros((N_A, bp, D_RNN), state_a_h.dtype)
    ha0 = jnp.zeros((N_A, bp, CONV_A - 1, D_RNN), cache_a_conv.dtype)
    hb0 = jnp.zeros((N_B, bp, CONV_B - 1, D_CONV), cache_b_conv.dtype)
    yp, p_state_a_h, p_cache_a_conv, p_cache_b_conv = trunk(xp, h0, ha0, hb0, p)
    y_prompt = yp[:, N_META:]

    y_sample, s_state_a_h, s_cache_a_conv, s_cache_b_conv = trunk(x_sample, state_a_h, cache_a_conv, cache_b_conv, p)

    return (y_prompt, y_sample, p_state_a_h, p_cache_a_conv, p_cache_b_conv, s_state_a_h, s_cache_a_conv, s_cache_b_conv)
```

```python
import functools

import jax
import jax.numpy as jnp
from jax import lax
from jax.experimental import pallas as pl
from jax.experimental.pallas import tpu as pltpu

F32 = jnp.float32
BF16 = jnp.bfloat16

EPS = 1e-6
RG_C = 8.0
RG_BLOCK = 128
N_META = 16

VMEM_LIMIT_BYTES = 56 * 1024 * 1024

ROW_TILE = 512
FF_TILE = 512
CH_TILE = 512
STAGE_PAD = 8


def _rms(x):
    return x * lax.rsqrt(jnp.mean(x * x, axis=-1, keepdims=True) + EPS)


def _dot(a, b):
    return jnp.dot(a, b, preferred_element_type=F32)


def _accumulate(o_ref, d, step):
    @pl.when(step == 0)
    def _():
        o_ref[...] = d

    @pl.when(step > 0)
    def _():
        o_ref[...] += d


def _ffn_kernel(x_ref, g_ref, wg_ref, wu_ref, wd_ref, o_ref, xn_ref):
    j = pl.program_id(1)

    @pl.when(j == 0)
    def _():
        xn_ref[...] = (_rms(x_ref[...]) * g_ref[0:1, :]).astype(BF16)

    xn = xn_ref[...]
    g = _dot(xn, wg_ref[...])
    u = _dot(xn, wu_ref[...])
    h = (g * jax.nn.sigmoid(g) * u).astype(BF16)
    _accumulate(o_ref, _dot(h, wd_ref[...]), j)

    @pl.when(j == pl.num_programs(1) - 1)
    def _():
        o_ref[...] = x_ref[...] + 0.5 * (_rms(o_ref[...]) * g_ref[1:2, :])


def _ffn(x, gains, wg, wu, wd, layer, row_tile):
    m, d = x.shape
    d_ff = wg.shape[-1]
    grid = (m // row_tile, d_ff // FF_TILE)
    return pl.pallas_call(
        _ffn_kernel,
        grid=grid,
        in_specs=[
            pl.BlockSpec((row_tile, d), lambda i, j: (i, 0)),
            pl.BlockSpec((2, d), lambda i, j: (0, 0)),
            pl.BlockSpec((None, d, FF_TILE), lambda i, j: (layer, 0, j)),
            pl.BlockSpec((None, d, FF_TILE), lambda i, j: (layer, 0, j)),
            pl.BlockSpec((None, FF_TILE, d), lambda i, j: (layer, j, 0)),
        ],
        out_specs=pl.BlockSpec((row_tile, d), lambda i, j: (i, 0)),
        out_shape=jax.ShapeDtypeStruct((m, d), F32),
        scratch_shapes=[pltpu.VMEM((row_tile, d), BF16)],
        compiler_params=pltpu.CompilerParams(
            dimension_semantics=("arbitrary", "arbitrary"),
            vmem_limit_bytes=VMEM_LIMIT_BYTES),
        name="ffn",
    )(x, gains, wg, wu, wd)


def _causal_conv(stage_ref, seq, hist, w_ref, width):
    n = seq.shape[0]
    hw = width - 1
    stage_ref[STAGE_PAD - hw:STAGE_PAD, :] = hist
    stage_ref[STAGE_PAD:STAGE_PAD + n, :] = seq
    out = seq * w_ref[hw:hw + 1, :]
    for k in range(hw):
        lo = STAGE_PAD - hw + k
        out = out + stage_ref[lo:lo + n, :] * w_ref[k:k + 1, :]
    return out, stage_ref[STAGE_PAD + n - hw:STAGE_PAD + n, :]


def _rglru_kernel(segments, carry,
                  x_ref, g_ref, wgate_ref, wu_ref, cw_ref, cb_ref, gw_ref, gb_ref,
                  lam_ref, wo_ref, h0_ref, hist0_ref,
                  o_ref, hlast_ref, histlast_ref,
                  xn_ref, state_ref, stage_ref, u_ref, a_ref, b_ref, h_ref):
    i = pl.program_id(0)
    c = pl.program_id(1)

    @pl.when(c == 0)
    def _():
        xn_ref[...] = (_rms(x_ref[...]) * g_ref[0:1, :]).astype(BF16)

    if carry:
        @pl.when(i == 0)
        def _():
            state_ref[c, 0:1, :] = h0_ref[0:1, :]
            state_ref[c, 1:4, :] = hist0_ref[0]

    xn = xn_ref[...]
    gate = _dot(xn, wgate_ref[...])
    u_pre = _dot(xn, wu_ref[...])

    for s, (r0, n) in enumerate(segments):
        hist = state_ref[c, 1:4, :] if carry else hist0_ref[s]
        conv, new_hist = _causal_conv(stage_ref, u_pre[r0:r0 + n, :], hist, cw_ref, 4)
        u_ref[r0:r0 + n, :] = conv + cb_ref[...]
        histlast_ref[s] = new_hist
        if carry:
            state_ref[c, 1:4, :] = new_hist

    lam = lam_ref[...]
    neg_softplus = -(jnp.maximum(-lam, 0.0) + jnp.log1p(jnp.exp(-jnp.abs(lam))))
    for nb in range(CH_TILE // RG_BLOCK):
        cols = slice(nb * RG_BLOCK, (nb + 1) * RG_BLOCK)
        ub = u_ref[:, cols]
        gm = _dot(ub.astype(BF16), gw_ref[nb]) + gb_ref[nb]
        r = jax.nn.sigmoid(gm[:, :RG_BLOCK])
        ig = jax.nn.sigmoid(gm[:, RG_BLOCK:])
        log_a = RG_C * r * neg_softplus[:, cols]
        a = jnp.exp(log_a)
        a_ref[:, cols] = a
        one_minus_a2 = -jnp.tanh(log_a) * (a * a + 1.0)
        b_ref[:, cols] = jnp.sqrt(one_minus_a2) * (ig * ub)

    for s, (r0, n) in enumerate(segments):
        h_init = state_ref[c, 0:1, :] if carry else h0_ref[s:s + 1, :]

        def step(t, h):
            h = a_ref[pl.ds(t, 1), :] * h + b_ref[pl.ds(t, 1), :]
            h_ref[pl.ds(t, 1), :] = h
            return h

        h_fin = lax.fori_loop(r0, r0 + n, step, h_init)
        hlast_ref[s:s + 1, :] = h_fin
        if carry:
            state_ref[c, 0:1, :] = h_fin

    y = (jax.nn.gelu(gate) * h_ref[...]).astype(BF16)
    _accumulate(o_ref, _dot(y, wo_ref[...]), c)

    @pl.when(c == pl.num_programs(1) - 1)
    def _():
        o_ref[...] = x_ref[...] + _rms(o_ref[...]) * g_ref[1:2, :]


def _rglru(x, gains, w_in, conv_w, conv_b, gate_w, gate_b, lam, w_out, h0, hist0,
           layer, row_tile, segments, carry):
    m, d = x.shape
    d_rnn = w_out.shape[-2]
    nc = d_rnn // CH_TILE
    nb = CH_TILE // RG_BLOCK
    n_seq = len(segments)
    grid = (m // row_tile, nc)
    kernel = functools.partial(_rglru_kernel, tuple(segments), carry)
    return pl.pallas_call(
        kernel,
        grid=grid,
        in_specs=[
            pl.BlockSpec((row_tile, d), lambda i, c: (i, 0)),
            pl.BlockSpec((2, d), lambda i, c: (0, 0)),
            pl.BlockSpec((None, d, CH_TILE), lambda i, c: (layer, 0, c)),
            pl.BlockSpec((None, d, CH_TILE), lambda i, c: (layer, 0, nc + c)),
            pl.BlockSpec((None, 4, CH_TILE), lambda i, c: (layer, 0, c)),
            pl.BlockSpec((None, 1, CH_TILE), lambda i, c: (layer, 0, c)),
            pl.BlockSpec((None, nb, RG_BLOCK, 2 * RG_BLOCK), lambda i, c: (layer, c, 0, 0)),
            pl.BlockSpec((None, nb, 1, 2 * RG_BLOCK), lambda i, c: (layer, c, 0, 0)),
            pl.BlockSpec((None, 1, CH_TILE), lambda i, c: (layer, 0, c)),
            pl.BlockSpec((None, CH_TILE, d), lambda i, c: (layer, c, 0)),
            pl.BlockSpec((n_seq, CH_TILE), lambda i, c: (0, c)),
            pl.BlockSpec((n_seq, 3, CH_TILE), lambda i, c: (0, 0, c)),
        ],
        out_specs=[
            pl.BlockSpec((row_tile, d), lambda i, c: (i, 0)),
            pl.BlockSpec((None, n_seq, CH_TILE), lambda i, c: (i, 0, c)),
            pl.BlockSpec((None, n_seq, 3, CH_TILE), lambda i, c: (i, 0, 0, c)),
        ],
        out_shape=[
            jax.ShapeDtypeStruct((m, d), F32),
            jax.ShapeDtypeStruct((grid[0], n_seq, d_rnn), F32),
            jax.ShapeDtypeStruct((grid[0], n_seq, 3, d_rnn), F32),
        ],
        scratch_shapes=[
            pltpu.VMEM((row_tile, d), BF16),
            pltpu.VMEM((nc, 8, CH_TILE), F32),
            pltpu.VMEM((row_tile + STAGE_PAD, CH_TILE), F32),
            pltpu.VMEM((row_tile, CH_TILE), F32),
            pltpu.VMEM((row_tile, CH_TILE), F32),
            pltpu.VMEM((row_tile, CH_TILE), F32),
            pltpu.VMEM((row_tile, CH_TILE), F32),
        ],
        compiler_params=pltpu.CompilerParams(
            dimension_semantics=("arbitrary", "arbitrary"),
            vmem_limit_bytes=VMEM_LIMIT_BYTES),
        name="rglru",
    )(x, gains, w_in, w_in, conv_w, conv_b, gate_w, gate_b, lam, w_out, h0, hist0)


def _shortconv_kernel(segments, carry,
                      x_ref, g_ref, wb_ref, wc_ref, wv_ref, cw_ref, wo_ref, hist0_ref,
                      o_ref, histlast_ref,
                      xn_ref, state_ref, stage_ref, y_ref):
    i = pl.program_id(0)
    c = pl.program_id(1)

    @pl.when(c == 0)
    def _():
        xn_ref[...] = (_rms(x_ref[...]) * g_ref[0:1, :]).astype(BF16)

    if carry:
        @pl.when(i == 0)
        def _():
            state_ref[c, 0:2, :] = hist0_ref[0]

    xn = xn_ref[...]
    gb = _dot(xn, wb_ref[...])
    cv = _dot(xn, wc_ref[...]) * _dot(xn, wv_ref[...])

    for s, (r0, n) in enumerate(segments):
        hist = state_ref[c, 0:2, :] if carry else hist0_ref[s]
        z, new_hist = _causal_conv(stage_ref, cv[r0:r0 + n, :], hist, cw_ref, 3)
        y_ref[r0:r0 + n, :] = (gb[r0:r0 + n, :] * z).astype(BF16)
        histlast_ref[s] = new_hist
        if carry:
            state_ref[c, 0:2, :] = new_hist

    _accumulate(o_ref, _dot(y_ref[...], wo_ref[...]), c)

    @pl.when(c == pl.num_programs(1) - 1)
    def _():
        o_ref[...] = x_ref[...] + _rms(o_ref[...]) * g_ref[1:2, :]


def _shortconv(x, gains, w_in, conv_w, w_out, hist0, layer, row_tile, segments, carry):
    m, d = x.shape
    d_conv = w_out.shape[-2]
    nc = d_conv // CH_TILE
    n_seq = len(segments)
    grid = (m // row_tile, nc)
    kernel = functools.partial(_shortconv_kernel, tuple(segments), carry)
    return pl.pallas_call(
        kernel,
        grid=grid,
        in_specs=[
            pl.BlockSpec((row_tile, d), lambda i, c: (i, 0)),
            pl.BlockSpec((2, d), lambda i, c: (0, 0)),
            pl.BlockSpec((None, d, CH_TILE), lambda i, c: (layer, 0, c)),
            pl.BlockSpec((None, d, CH_TILE), lambda i, c: (layer, 0, nc + c)),
            pl.BlockSpec((None, d, CH_TILE), lambda i, c: (layer, 0, 2 * nc + c)),
            pl.BlockSpec((None, 3, CH_TILE), lambda i, c: (layer, 0, c)),
            pl.BlockSpec((None, CH_TILE, d), lambda i, c: (layer, c, 0)),
            pl.BlockSpec((n_seq, 2, CH_TILE), lambda i, c: (0, 0, c)),
        ],
        out_specs=[
            pl.BlockSpec((row_tile, d), lambda i, c: (i, 0)),
            pl.BlockSpec((None, n_seq, 2, CH_TILE), lambda i, c: (i, 0, 0, c)),
        ],
        out_shape=[
            jax.ShapeDtypeStruct((m, d), F32),
            jax.ShapeDtypeStruct((grid[0], n_seq, 2, d_conv), F32),
        ],
        scratch_shapes=[
            pltpu.VMEM((row_tile, d), BF16),
            pltpu.VMEM((nc, 8, CH_TILE), F32),
            pltpu.VMEM((row_tile + STAGE_PAD, CH_TILE), F32),
            pltpu.VMEM((row_tile, CH_TILE), BF16),
        ],
        compiler_params=pltpu.CompilerParams(
            dimension_semantics=("arbitrary", "arbitrary"),
            vmem_limit_bytes=VMEM_LIMIT_BYTES),
        name="shortconv",
    )(x, gains, w_in, w_in, w_in, conv_w, w_out, hist0)


def kernel(x_prompt, x_sample, state_a_h, cache_a_conv, cache_b_conv, meta_tokens, norm_g, ffn1_wg, ffn1_wu, ffn1_wd, ffn2_wg, ffn2_wu, ffn2_wd, a_w_in, a_conv_w, a_conv_b, a_gate_w, a_gate_b, a_lambda, a_w_out, b_w_in, b_conv_w, b_w_out):
    depth = norm_g.shape[0]
    n_prompt, seq, d = x_prompt.shape
    n_dec, dec_seq, _ = x_sample.shape
    assert n_prompt == 1, "prompt rows are treated as one causal sequence"
    assert seq % ROW_TILE == 0

    x_main = x_prompt.reshape(seq, d)
    x_small = jnp.concatenate(
        [meta_tokens.astype(F32), x_sample.reshape(n_dec * dec_seq, d)], axis=0)
    m_small = x_small.shape[0]
    small_segments = [(0, N_META)] + [(N_META + b * dec_seq, dec_seq) for b in range(n_dec)]
    main_segments = [(0, ROW_TILE)]

    bf = lambda w: w.astype(BF16)
    ffn_w = [(bf(ffn1_wg), bf(ffn1_wu), bf(ffn1_wd)), (bf(ffn2_wg), bf(ffn2_wu), bf(ffn2_wd))]
    a_w_in_b, a_w_out_b, a_gate_w_b = bf(a_w_in), bf(a_w_out), bf(a_gate_w)
    b_w_in_b, b_w_out_b = bf(b_w_in), bf(b_w_out)
    a_conv_b3 = a_conv_b[:, None, :]
    a_lambda3 = a_lambda[:, None, :]
    a_gate_b4 = a_gate_b[:, :, None, :]

    p_h, p_ha, p_hb, s_h, s_ha, s_hb = [], [], [], [], [], []
    for layer in range(depth):
        g = norm_g[layer]
        j = layer // 2

        x_main = _ffn(x_main, g[0:2], *ffn_w[0], layer, ROW_TILE)
        x_small = _ffn(x_small, g[0:2], *ffn_w[0], layer, m_small)

        if layer % 2 == 0:
            h0 = jnp.concatenate([jnp.zeros((1, state_a_h.shape[-1]), F32), state_a_h[j]], axis=0)
            hist0 = jnp.concatenate(
                [jnp.zeros((1,) + cache_a_conv.shape[2:], F32), cache_a_conv[j]], axis=0)
            a_args = (a_w_in_b, a_conv_w, a_conv_b3, a_gate_w_b, a_gate_b4, a_lambda3, a_w_out_b)
            x_small, h_s, hist_s = _rglru(x_small, g[2:4], *a_args, h0, hist0,
                                          j, m_small, small_segments, False)
            h_s, hist_s = h_s[0], hist_s[0]
            x_main, h_m, hist_m = _rglru(x_main, g[2:4], *a_args, h_s[0:1], hist_s[0:1],
                                         j, ROW_TILE, main_segments, True)
            p_h.append(h_m[-1])
            p_ha.append(hist_m[-1])
            s_h.append(h_s[1:])
            s_ha.append(hist_s[1:])
        else:
            hist0 = jnp.concatenate(
                [jnp.zeros((1,) + cache_b_conv.shape[2:], F32), cache_b_conv[j]], axis=0)
            b_args = (b_w_in_b, b_conv_w, b_w_out_b)
            x_small, hist_s = _shortconv(x_small, g[2:4], *b_args, hist0,
                                         j, m_small, small_segments, False)
            hist_s = hist_s[0]
            x_main, hist_m = _shortconv(x_main, g[2:4], *b_args, hist_s[0:1],
                                        j, ROW_TILE, main_segments, True)
            p_hb.append(hist_m[-1])
            s_hb.append(hist_s[1:])

        x_main = _ffn(x_main, g[4:6], *ffn_w[1], layer, ROW_TILE)
        x_small = _ffn(x_small, g[4:6], *ffn_w[1], layer, m_small)

    y_prompt = x_main.reshape(1, seq, d)
    y_sample = x_small[N_META:].reshape(n_dec, dec_seq, d)
    return (y_prompt, y_sample,
            jnp.stack(p_h), jnp.stack(p_ha), jnp.stack(p_hb),
            jnp.stack(s_h), jnp.stack(s_ha), jnp.stack(s_hb))
```

```python
import functools

import jax
import jax.numpy as jnp
from jax import lax
from jax.experimental import pallas as pl
from jax.experimental.pallas import tpu as pltpu

F32 = jnp.float32
BF16 = jnp.bfloat16

EPS = 1e-6
RG_C = 8.0
RG_BLOCK = 128
N_META = 16
CONV_A = 4
CONV_B = 3

LANES = 128
SUBLANES = 8
VMEM_LIMIT_BYTES = 56 * 1024 * 1024

ROW_TILE = 512
FF_TILE = 512
CH_TILE = 512
NORM_ROWS = 16
CONV_ROWS = 64
LOOP_UNROLL = 4


def _dot(a, b):
    return jnp.dot(a, b, preferred_element_type=F32)


def _row_chunks(n_rows, rows_per_chunk, body, unroll=LOOP_UNROLL):
    def step(k, _):
        body(pl.multiple_of(k * rows_per_chunk, rows_per_chunk))
        return 0
    lax.fori_loop(0, n_rows // rows_per_chunk, step, 0, unroll=unroll)


def _lane_tile(v, d):
    return jnp.concatenate([v] * (d // LANES), axis=-1)


def _inv_rms_to(src_ref, scale, ss_ref, inv_ref):
    n_rows, d = src_ref.shape

    def partial(r0):
        x = src_ref[pl.ds(r0, SUBLANES), :]
        x2 = x * x
        acc = x2[:, 0:LANES]
        for q in range(1, d // LANES):
            acc = acc + x2[:, q * LANES:(q + 1) * LANES]
        ss_ref[pl.ds(r0, SUBLANES), :] = acc
    _row_chunks(n_rows, SUBLANES, partial)
    tot = jnp.sum(ss_ref[...], axis=-1, keepdims=True)
    inv = scale * lax.rsqrt(tot * (1.0 / d) + EPS)
    inv_ref[...] = jnp.broadcast_to(inv, inv_ref.shape)


def _prenorm(x_ref, g_ref, xn_ref, ss_ref, inv_ref):
    n_rows, d = x_ref.shape
    _inv_rms_to(x_ref, 1.0, ss_ref, inv_ref)

    def apply(r0):
        rows = pl.ds(r0, NORM_ROWS)
        inv = _lane_tile(inv_ref[rows, :], d)
        xn_ref[rows, :] = ((x_ref[rows, :] * inv) * g_ref[0:1, :]).astype(BF16)
    _row_chunks(n_rows, NORM_ROWS, apply)


def _postnorm_residual(x_ref, g_ref, o_ref, scale, ss_ref, inv_ref):
    n_rows, d = x_ref.shape
    _inv_rms_to(o_ref, scale, ss_ref, inv_ref)

    def apply(r0):
        rows = pl.ds(r0, SUBLANES)
        inv = _lane_tile(inv_ref[rows, :], d)
        o_ref[rows, :] = x_ref[rows, :] + (o_ref[rows, :] * inv) * g_ref[1:2, :]
    _row_chunks(n_rows, SUBLANES, apply)


def _compiler_params():
    return pltpu.CompilerParams(
        dimension_semantics=("arbitrary", "arbitrary"),
        vmem_limit_bytes=VMEM_LIMIT_BYTES)


def _ffn_kernel(x_ref, g_ref, wg_ref, wu_ref, wd_ref, o_ref,
                xn_ref, gate_ref, up_ref, h_ref, ss_ref, inv_ref):
    j = pl.program_id(1)

    @pl.when(j == 0)
    def _():
        _prenorm(x_ref, g_ref, xn_ref, ss_ref, inv_ref)
        o_ref[...] = jnp.zeros_like(o_ref)

    xn = xn_ref[...]
    gate_ref[...] = _dot(xn, wg_ref[...])
    up_ref[...] = _dot(xn, wu_ref[...])
    g = gate_ref[...]
    h_ref[...] = (g * jax.nn.sigmoid(g) * up_ref[...]).astype(BF16)
    o_ref[...] += _dot(h_ref[...], wd_ref[...])

    @pl.when(j == pl.num_programs(1) - 1)
    def _():
        _postnorm_residual(x_ref, g_ref, o_ref, 0.5, ss_ref, inv_ref)


def _ffn(x, gains, wg, wu, wd, layer, row_tile):
    m, d = x.shape
    d_ff = wg.shape[-1]
    grid = (m // row_tile, d_ff // FF_TILE)
    return pl.pallas_call(
        _ffn_kernel,
        grid=grid,
        in_specs=[
            pl.BlockSpec((row_tile, d), lambda i, j: (i, 0)),
            pl.BlockSpec((2, d), lambda i, j: (0, 0)),
            pl.BlockSpec((None, d, FF_TILE), lambda i, j: (layer, 0, j)),
            pl.BlockSpec((None, d, FF_TILE), lambda i, j: (layer, 0, j)),
            pl.BlockSpec((None, FF_TILE, d), lambda i, j: (layer, j, 0)),
        ],
        out_specs=pl.BlockSpec((row_tile, d), lambda i, j: (i, 0)),
        out_shape=jax.ShapeDtypeStruct((m, d), F32),
        scratch_shapes=[
            pltpu.VMEM((row_tile, d), BF16),
            pltpu.VMEM((row_tile, FF_TILE), F32),
            pltpu.VMEM((row_tile, FF_TILE), F32),
            pltpu.VMEM((row_tile, FF_TILE), BF16),
            pltpu.VMEM((row_tile, LANES), F32),
            pltpu.VMEM((row_tile, LANES), F32),
        ],
        compiler_params=_compiler_params(),
        name="ffn",
    )(x, gains, wg, wu, wd)


def _neg_softplus_neg(lam):
    return -(jnp.maximum(-lam, 0.0) + jnp.log1p(jnp.exp(-jnp.abs(lam))))


def _rglru_coeffs(r_pre, i_pre, u, log_sig_lam):
    r = jax.nn.sigmoid(r_pre)
    ig = jax.nn.sigmoid(i_pre)
    log_a = RG_C * r * log_sig_lam
    a = jnp.exp(log_a)
    one_minus_a2 = -jnp.tanh(log_a) * (a * a + 1.0)
    return a, jnp.sqrt(one_minus_a2) * (ig * u)


def _conv_taps(stage_ref, w_ref, r0, rows, width):
    hw = width - 1
    out = stage_ref[SUBLANES + r0:SUBLANES + r0 + rows, :] * w_ref[hw:hw + 1, :]
    for q in range(hw):
        lo = SUBLANES - hw + q + r0
        out = out + stage_ref[lo:lo + rows, :] * w_ref[q:q + 1, :]
    return out


def _rglru_main_kernel(x_ref, g_ref, wgate_ref, wu_ref, cw_ref, cb_ref, gw_ref, gb_ref,
                       lam_ref, wo_ref, h0_ref, hist0_ref,
                       o_ref, hlast_ref, histlast_ref,
                       xn_ref, state_ref, gate_ref, stage_ref, u_ref, gm_ref, y_ref,
                       ss_ref, inv_ref):
    i = pl.program_id(0)
    c = pl.program_id(1)
    n_rows = x_ref.shape[0]
    hw = CONV_A - 1
    n_blocks = CH_TILE // RG_BLOCK
    block_cols = [slice(nb * RG_BLOCK, (nb + 1) * RG_BLOCK) for nb in range(n_blocks)]

    @pl.when(c == 0)
    def _():
        _prenorm(x_ref, g_ref, xn_ref, ss_ref, inv_ref)
        o_ref[...] = jnp.zeros_like(o_ref)

    @pl.when(i == 0)
    def _():
        state_ref[c, 0:1, :] = h0_ref[0:1, :]
        state_ref[c, 1:1 + hw, :] = hist0_ref[0]

    xn = xn_ref[...]
    gate_ref[...] = _dot(xn, wgate_ref[...])
    stage_ref[SUBLANES:SUBLANES + n_rows, :] = _dot(xn, wu_ref[...])

    stage_ref[SUBLANES - hw:SUBLANES, :] = state_ref[c, 1:1 + hw, :]
    new_hist = stage_ref[SUBLANES + n_rows - hw:SUBLANES + n_rows, :]
    state_ref[c, 1:1 + hw, :] = new_hist
    histlast_ref[0] = new_hist
    for r0 in range(0, n_rows, CONV_ROWS):
        u_ref[r0:r0 + CONV_ROWS, :] = (
            _conv_taps(stage_ref, cw_ref, r0, CONV_ROWS, CONV_A) + cb_ref[...])

    for nb in range(n_blocks):
        ub = u_ref[:, block_cols[nb]].astype(BF16)
        gm_ref[:, 2 * nb * RG_BLOCK:2 * (nb + 1) * RG_BLOCK] = _dot(ub, gw_ref[nb])

    log_sig_lam = _neg_softplus_neg(lam_ref[...])
    sub = lax.broadcasted_iota(jnp.int32, (SUBLANES, RG_BLOCK), 0)

    def slab_scan(a, b, h_prev):
        for dist in (1, 2, 4):
            a_up = jnp.where(sub >= dist, pltpu.roll(a, dist, 0), 1.0)
            b_up = jnp.where(sub >= dist, pltpu.roll(b, dist, 0), 0.0)
            b = a * b_up + b
            a = a * a_up
        return b + a * h_prev

    def rec_chunk(k, h_prev):
        r0 = pl.multiple_of(k * NORM_ROWS, NORM_ROWS)
        rows = pl.ds(r0, NORM_ROWS)
        h_next = []
        for nb in range(n_blocks):
            cols = block_cols[nb]
            g0 = 2 * nb * RG_BLOCK
            bias = gb_ref[nb]
            a, b = _rglru_coeffs(
                gm_ref[rows, g0:g0 + RG_BLOCK] + bias[:, :RG_BLOCK],
                gm_ref[rows, g0 + RG_BLOCK:g0 + 2 * RG_BLOCK] + bias[:, RG_BLOCK:],
                u_ref[rows, cols], log_sig_lam[:, cols])
            h_lo = slab_scan(a[:SUBLANES], b[:SUBLANES], h_prev[nb])
            h_hi = slab_scan(a[SUBLANES:], b[SUBLANES:], h_lo[SUBLANES - 1:SUBLANES, :])
            h = jnp.concatenate([h_lo, h_hi], axis=0)
            y_ref[rows, cols] = (jax.nn.gelu(gate_ref[rows, cols]) * h).astype(BF16)
            h_next.append(h_hi[SUBLANES - 1:SUBLANES, :])
        return tuple(h_next)

    h_in = state_ref[c, 0:1, :]
    h_out = lax.fori_loop(0, n_rows // NORM_ROWS, rec_chunk,
                          tuple(h_in[:, cols] for cols in block_cols), unroll=2)
    h_out = jnp.concatenate(h_out, axis=-1)
    state_ref[c, 0:1, :] = h_out
    hlast_ref[0:1, :] = h_out

    o_ref[...] += _dot(y_ref[...], wo_ref[...])

    @pl.when(c == pl.num_programs(1) - 1)
    def _():
        _postnorm_residual(x_ref, g_ref, o_ref, 1.0, ss_ref, inv_ref)


def _rglru(x, gains, w_in, conv_w, conv_b, gate_w, gate_b, lam, w_out, h0, hist0,
           layer, row_tile, segments):
    m, d = x.shape
    d_rnn = w_out.shape[-2]
    nc = d_rnn // CH_TILE
    nb = CH_TILE // RG_BLOCK
    hw = CONV_A - 1
    n_seq = 1 if segments is None else len(segments)
    grid = (m // row_tile, nc)
    tile_f32 = pltpu.VMEM((row_tile, CH_TILE), F32)
    norm_scratch = [pltpu.VMEM((row_tile, LANES), F32)] * 2
    if segments is None:
        kernel = _rglru_main_kernel
        scratch = [
            pltpu.VMEM((row_tile, d), BF16),
            pltpu.VMEM((nc, SUBLANES, CH_TILE), F32),
            tile_f32,
            pltpu.VMEM((row_tile + SUBLANES, CH_TILE), F32),
            tile_f32,
            pltpu.VMEM((row_tile, 2 * CH_TILE), F32),
            pltpu.VMEM((row_tile, CH_TILE), BF16),
        ] + norm_scratch
    else:
        kernel = functools.partial(_rglru_small_kernel, tuple(segments))
        scratch = [
            pltpu.VMEM((row_tile, d), BF16),
            pltpu.VMEM((row_tile + SUBLANES, CH_TILE), F32),
            tile_f32, tile_f32, tile_f32, tile_f32,
        ] + norm_scratch
    return pl.pallas_call(
        kernel,
        grid=grid,
        in_specs=[
            pl.BlockSpec((row_tile, d), lambda i, c: (i, 0)),
            pl.BlockSpec((2, d), lambda i, c: (0, 0)),
            pl.BlockSpec((None, d, CH_TILE), lambda i, c: (layer, 0, c)),
            pl.BlockSpec((None, d, CH_TILE), lambda i, c: (layer, 0, nc + c)),
            pl.BlockSpec((None, CONV_A, CH_TILE), lambda i, c: (layer, 0, c)),
            pl.BlockSpec((None, 1, CH_TILE), lambda i, c: (layer, 0, c)),
            pl.BlockSpec((None, nb, RG_BLOCK, 2 * RG_BLOCK), lambda i, c: (layer, c, 0, 0)),
            pl.BlockSpec((None, nb, 1, 2 * RG_BLOCK), lambda i, c: (layer, c, 0, 0)),
            pl.BlockSpec((None, 1, CH_TILE), lambda i, c: (layer, 0, c)),
            pl.BlockSpec((None, CH_TILE, d), lambda i, c: (layer, c, 0)),
            pl.BlockSpec((n_seq, CH_TILE), lambda i, c: (0, c)),
            pl.BlockSpec((n_seq, hw, CH_TILE), lambda i, c: (0, 0, c)),
        ],
        out_specs=[
            pl.BlockSpec((row_tile, d), lambda i, c: (i, 0)),
            pl.BlockSpec((None, n_seq, CH_TILE), lambda i, c: (i, 0, c)),
            pl.BlockSpec((None, n_seq, hw, CH_TILE), lambda i, c: (i, 0, 0, c)),
        ],
        out_shape=[
            jax.ShapeDtypeStruct((m, d), F32),
            jax.ShapeDtypeStruct((grid[0], n_seq, d_rnn), F32),
            jax.ShapeDtypeStruct((grid[0], n_seq, hw, d_rnn), F32),
        ],
        scratch_shapes=scratch,
        compiler_params=_compiler_params(),
        name="rglru",
    )(x, gains, w_in, w_in, conv_w, conv_b, gate_w, gate_b, lam, w_out, h0, hist0)


def _segment_conv(stage_ref, seq, hist, w_ref, width):
    n = seq.shape[0]
    hw = width - 1
    stage_ref[SUBLANES - hw:SUBLANES, :] = hist
    stage_ref[SUBLANES:SUBLANES + n, :] = seq
    return (_conv_taps(stage_ref, w_ref, 0, n, width),
            stage_ref[SUBLANES + n - hw:SUBLANES + n, :])


def _rglru_small_kernel(segments,
                        x_ref, g_ref, wgate_ref, wu_ref, cw_ref, cb_ref, gw_ref, gb_ref,
                        lam_ref, wo_ref, h0_ref, hist0_ref,
                        o_ref, hlast_ref, histlast_ref,
                        xn_ref, stage_ref, u_ref, a_ref, b_ref, h_ref, ss_ref, inv_ref):
    c = pl.program_id(1)

    @pl.when(c == 0)
    def _():
        _prenorm(x_ref, g_ref, xn_ref, ss_ref, inv_ref)
        o_ref[...] = jnp.zeros_like(o_ref)

    xn = xn_ref[...]
    gate = _dot(xn, wgate_ref[...])
    u_pre = _dot(xn, wu_ref[...])

    for s, (r0, n) in enumerate(segments):
        conv, new_hist = _segment_conv(stage_ref, u_pre[r0:r0 + n, :], hist0_ref[s], cw_ref, CONV_A)
        u_ref[r0:r0 + n, :] = conv + cb_ref[...]
        histlast_ref[s] = new_hist

    log_sig_lam = _neg_softplus_neg(lam_ref[...])
    for nb in range(CH_TILE // RG_BLOCK):
        cols = slice(nb * RG_BLOCK, (nb + 1) * RG_BLOCK)
        ub = u_ref[:, cols]
        gm = _dot(ub.astype(BF16), gw_ref[nb]) + gb_ref[nb]
        a, b = _rglru_coeffs(gm[:, :RG_BLOCK], gm[:, RG_BLOCK:], ub, log_sig_lam[:, cols])
        a_ref[:, cols] = a
        b_ref[:, cols] = b

    for s, (r0, n) in enumerate(segments):
        def step(t, h):
            h = a_ref[pl.ds(t, 1), :] * h + b_ref[pl.ds(t, 1), :]
            h_ref[pl.ds(t, 1), :] = h
            return h
        hlast_ref[s:s + 1, :] = lax.fori_loop(r0, r0 + n, step, h0_ref[s:s + 1, :])

    y = (jax.nn.gelu(gate) * h_ref[...]).astype(BF16)
    o_ref[...] += _dot(y, wo_ref[...])

    @pl.when(c == pl.num_programs(1) - 1)
    def _():
        _postnorm_residual(x_ref, g_ref, o_ref, 1.0, ss_ref, inv_ref)


def _shortconv_small_kernel(segments,
                            x_ref, g_ref, wb_ref, wc_ref, wv_ref, cw_ref, wo_ref, hist0_ref,
                            o_ref, histlast_ref,
                            xn_ref, stage_ref, y_ref, ss_ref, inv_ref):
    c = pl.program_id(1)

    @pl.when(c == 0)
    def _():
        _prenorm(x_ref, g_ref, xn_ref, ss_ref, inv_ref)
        o_ref[...] = jnp.zeros_like(o_ref)

    xn = xn_ref[...]
    gb = _dot(xn, wb_ref[...])
    cv = _dot(xn, wc_ref[...]) * _dot(xn, wv_ref[...])

    for s, (r0, n) in enumerate(segments):
        z, new_hist = _segment_conv(stage_ref, cv[r0:r0 + n, :], hist0_ref[s], cw_ref, CONV_B)
        y_ref[r0:r0 + n, :] = (gb[r0:r0 + n, :] * z).astype(BF16)
        histlast_ref[s] = new_hist

    o_ref[...] += _dot(y_ref[...], wo_ref[...])

    @pl.when(c == pl.num_programs(1) - 1)
    def _():
        _postnorm_residual(x_ref, g_ref, o_ref, 1.0, ss_ref, inv_ref)


def _shortconv_main_kernel(x_ref, g_ref, wb_ref, wc_ref, wv_ref, cw_ref, wo_ref, hist0_ref,
                           o_ref, histlast_ref,
                           xn_ref, state_ref, b_ref, c_ref, stage_ref, y_ref, ss_ref, inv_ref):
    i = pl.program_id(0)
    c = pl.program_id(1)
    n_rows = x_ref.shape[0]
    hw = CONV_B - 1

    @pl.when(c == 0)
    def _():
        _prenorm(x_ref, g_ref, xn_ref, ss_ref, inv_ref)
        o_ref[...] = jnp.zeros_like(o_ref)

    @pl.when(i == 0)
    def _():
        state_ref[c, 0:hw, :] = hist0_ref[0]

    xn = xn_ref[...]
    b_ref[...] = _dot(xn, wb_ref[...])
    c_ref[...] = _dot(xn, wc_ref[...])
    stage_ref[SUBLANES:SUBLANES + n_rows, :] = _dot(xn, wv_ref[...])

    def cv_chunk(r0):
        rows = pl.ds(pl.multiple_of(SUBLANES + r0, SUBLANES), NORM_ROWS)
        stage_ref[rows, :] = stage_ref[rows, :] * c_ref[pl.ds(r0, NORM_ROWS), :]
    _row_chunks(n_rows, NORM_ROWS, cv_chunk)

    stage_ref[SUBLANES - hw:SUBLANES, :] = state_ref[c, 0:hw, :]
    new_hist = stage_ref[SUBLANES + n_rows - hw:SUBLANES + n_rows, :]
    state_ref[c, 0:hw, :] = new_hist
    histlast_ref[0] = new_hist

    for r0 in range(0, n_rows, CONV_ROWS):
        z = _conv_taps(stage_ref, cw_ref, r0, CONV_ROWS, CONV_B)
        y_ref[r0:r0 + CONV_ROWS, :] = (b_ref[r0:r0 + CONV_ROWS, :] * z).astype(BF16)

    o_ref[...] += _dot(y_ref[...], wo_ref[...])

    @pl.when(c == pl.num_programs(1) - 1)
    def _():
        _postnorm_residual(x_ref, g_ref, o_ref, 1.0, ss_ref, inv_ref)


def _shortconv(x, gains, w_in, conv_w, w_out, hist0, layer, row_tile, segments):
    m, d = x.shape
    d_conv = w_out.shape[-2]
    nc = d_conv // CH_TILE
    hw = CONV_B - 1
    n_seq = 1 if segments is None else len(segments)
    grid = (m // row_tile, nc)
    tile_f32 = pltpu.VMEM((row_tile, CH_TILE), F32)
    norm_scratch = [pltpu.VMEM((row_tile, LANES), F32)] * 2
    if segments is None:
        kernel = _shortconv_main_kernel
        scratch = [
            pltpu.VMEM((row_tile, d), BF16),
            pltpu.VMEM((nc, SUBLANES, CH_TILE), F32),
            tile_f32, tile_f32,
            pltpu.VMEM((row_tile + SUBLANES, CH_TILE), F32),
            pltpu.VMEM((row_tile, CH_TILE), BF16),
        ] + norm_scratch
    else:
        kernel = functools.partial(_shortconv_small_kernel, tuple(segments))
        scratch = [
            pltpu.VMEM((row_tile, d), BF16),
            pltpu.VMEM((row_tile + SUBLANES, CH_TILE), F32),
            pltpu.VMEM((row_tile, CH_TILE), BF16),
        ] + norm_scratch
    return pl.pallas_call(
        kernel,
        grid=grid,
        in_specs=[
            pl.BlockSpec((row_tile, d), lambda i, c: (i, 0)),
            pl.BlockSpec((2, d), lambda i, c: (0, 0)),
            pl.BlockSpec((None, d, CH_TILE), lambda i, c: (layer, 0, c)),
            pl.BlockSpec((None, d, CH_TILE), lambda i, c: (layer, 0, nc + c)),
            pl.BlockSpec((None, d, CH_TILE), lambda i, c: (layer, 0, 2 * nc + c)),
            pl.BlockSpec((None, CONV_B, CH_TILE), lambda i, c: (layer, 0, c)),
            pl.BlockSpec((None, CH_TILE, d), lambda i, c: (layer, c, 0)),
            pl.BlockSpec((n_seq, hw, CH_TILE), lambda i, c: (0, 0, c)),
        ],
        out_specs=[
            pl.BlockSpec((row_tile, d), lambda i, c: (i, 0)),
            pl.BlockSpec((None, n_seq, hw, CH_TILE), lambda i, c: (i, 0, 0, c)),
        ],
        out_shape=[
            jax.ShapeDtypeStruct((m, d), F32),
            jax.ShapeDtypeStruct((grid[0], n_seq, hw, d_conv), F32),
        ],
        scratch_shapes=scratch,
        compiler_params=_compiler_params(),
        name="shortconv",
    )(x, gains, w_in, w_in, w_in, conv_w, w_out, hist0)


def kernel(x_prompt, x_sample, state_a_h, cache_a_conv, cache_b_conv, meta_tokens, norm_g, ffn1_wg, ffn1_wu, ffn1_wd, ffn2_wg, ffn2_wu, ffn2_wd, a_w_in, a_conv_w, a_conv_b, a_gate_w, a_gate_b, a_lambda, a_w_out, b_w_in, b_conv_w, b_w_out):
    depth = norm_g.shape[0]
    n_prompt, seq, d = x_prompt.shape
    n_dec, dec_seq, _ = x_sample.shape
    assert n_prompt == 1, "prompt rows are treated as one causal sequence"
    assert seq % ROW_TILE == 0

    x_main = x_prompt.reshape(seq, d)
    x_small = jnp.concatenate(
        [meta_tokens.astype(F32), x_sample.reshape(n_dec * dec_seq, d)], axis=0)
    m_small = x_small.shape[0]
    small_segments = [(0, N_META)] + [(N_META + b * dec_seq, dec_seq) for b in range(n_dec)]

    bf = lambda w: w.astype(BF16)
    ffn_w = [(bf(ffn1_wg), bf(ffn1_wu), bf(ffn1_wd)), (bf(ffn2_wg), bf(ffn2_wu), bf(ffn2_wd))]
    a_w_in_b, a_w_out_b, a_gate_w_b = bf(a_w_in), bf(a_w_out), bf(a_gate_w)
    b_w_in_b, b_w_out_b = bf(b_w_in), bf(b_w_out)
    a_conv_b3 = a_conv_b[:, None, :]
    a_lambda3 = a_lambda[:, None, :]
    a_gate_b4 = a_gate_b[:, :, None, :]

    p_h, p_ha, p_hb, s_h, s_ha, s_hb = [], [], [], [], [], []
    for layer in range(depth):
        g = norm_g[layer]
        j = layer // 2

        x_main = _ffn(x_main, g[0:2], *ffn_w[0], layer, ROW_TILE)
        x_small = _ffn(x_small, g[0:2], *ffn_w[0], layer, m_small)

        if layer % 2 == 0:
            h0 = jnp.concatenate([jnp.zeros((1, state_a_h.shape[-1]), F32), state_a_h[j]], axis=0)
            hist0 = jnp.concatenate(
                [jnp.zeros((1,) + cache_a_conv.shape[2:], F32), cache_a_conv[j]], axis=0)
            a_args = (a_w_in_b, a_conv_w, a_conv_b3, a_gate_w_b, a_gate_b4, a_lambda3, a_w_out_b)
            x_small, h_s, hist_s = _rglru(x_small, g[2:4], *a_args, h0, hist0,
                                          j, m_small, small_segments)
            h_s, hist_s = h_s[0], hist_s[0]
            x_main, h_m, hist_m = _rglru(x_main, g[2:4], *a_args, h_s[0:1], hist_s[0:1],
                                         j, ROW_TILE, None)
            p_h.append(h_m[-1])
            p_ha.append(hist_m[-1])
            s_h.append(h_s[1:])
            s_ha.append(hist_s[1:])
        else:
            hist0 = jnp.concatenate(
                [jnp.zeros((1,) + cache_b_conv.shape[2:], F32), cache_b_conv[j]], axis=0)
            b_args = (b_w_in_b, b_conv_w, b_w_out_b)
            x_small, hist_s = _shortconv(x_small, g[2:4], *b_args, hist0,
                                         j, m_small, small_segments)
            hist_s = hist_s[0]
            x_main, hist_m = _shortconv(x_main, g[2:4], *b_args, hist_s[0:1],
                                        j, ROW_TILE, None)
            p_hb.append(hist_m[-1])
            s_hb.append(hist_s[1:])

        x_main = _ffn(x_main, g[4:6], *ffn_w[1], layer, ROW_TILE)
        x_small = _ffn(x_small, g[4:6], *ffn_w[1], layer, m_small)

    y_prompt = x_main.reshape(1, seq, d)
    y_sample = x_small[N_META:].reshape(n_dec, dec_seq, d)
    return (y_prompt, y_sample,
            jnp.stack(p_h), jnp.stack(p_ha), jnp.stack(p_hb),
            jnp.stack(s_h), jnp.stack(s_ha), jnp.stack(s_hb))
```

```python
import functools

import jax
import jax.numpy as jnp
from jax import lax
from jax.experimental import pallas as pl
from jax.experimental.pallas import tpu as pltpu

F32 = jnp.float32
BF16 = jnp.bfloat16

EPS = 1e-6
RG_C = 8.0
RG_BLOCK = 128
N_META = 16
CONV_A = 4
CONV_B = 3

LANES = 128
SUBLANES = 8
MXU_COLS = 256
VMEM_LIMIT_BYTES = 56 * 1024 * 1024

ROW_TILE = 512
FF_TILE = 512
CH_TILE = 512
NORM_ROWS = 16
CONV_ROWS = 64
LOOP_UNROLL = 4


def _dot(a, b):
    return jnp.dot(a, b, preferred_element_type=F32)


def _row_chunks(n_rows, rows_per_chunk, body, unroll=LOOP_UNROLL):
    def step(k, _):
        body(pl.multiple_of(k * rows_per_chunk, rows_per_chunk))
        return 0
    lax.fori_loop(0, n_rows // rows_per_chunk, step, 0, unroll=unroll)


def _lane_tile(v, d):
    return jnp.concatenate([v] * (d // LANES), axis=-1)


def _inv_rms_to(src_ref, scale, ss_ref, inv_ref):
    n_rows, d = src_ref.shape

    def partial(r0):
        x = src_ref[pl.ds(r0, SUBLANES), :]
        x2 = x * x
        acc = x2[:, 0:LANES]
        for q in range(1, d // LANES):
            acc = acc + x2[:, q * LANES:(q + 1) * LANES]
        ss_ref[pl.ds(r0, SUBLANES), :] = acc
    _row_chunks(n_rows, SUBLANES, partial)
    tot = jnp.sum(ss_ref[...], axis=-1, keepdims=True)
    inv = scale * lax.rsqrt(tot * (1.0 / d) + EPS)
    inv_ref[...] = jnp.broadcast_to(inv, inv_ref.shape)


def _prenorm(x_ref, g_ref, xn_ref, ss_ref, inv_ref):
    n_rows, d = x_ref.shape
    _inv_rms_to(x_ref, 1.0, ss_ref, inv_ref)
    gain = jnp.broadcast_to(g_ref[0:1, :], (SUBLANES, d))
    gain = jnp.concatenate([gain] * (NORM_ROWS // SUBLANES), axis=0)

    def apply(r0):
        rows = pl.ds(r0, NORM_ROWS)
        inv = _lane_tile(inv_ref[rows, :], d)
        xn_ref[rows, :] = ((x_ref[rows, :] * inv) * gain).astype(BF16)
    _row_chunks(n_rows, NORM_ROWS, apply)


def _postnorm_residual(x_ref, g_ref, o_ref, scale, ss_ref, inv_ref):
    n_rows, d = x_ref.shape
    _inv_rms_to(o_ref, scale, ss_ref, inv_ref)
    gain = jnp.broadcast_to(g_ref[1:2, :], (SUBLANES, d))

    def apply(r0):
        rows = pl.ds(r0, SUBLANES)
        inv = _lane_tile(inv_ref[rows, :], d)
        o_ref[rows, :] = x_ref[rows, :] + (o_ref[rows, :] * inv) * gain
    _row_chunks(n_rows, SUBLANES, apply)


def _compiler_params():
    return pltpu.CompilerParams(
        dimension_semantics=("arbitrary", "arbitrary"),
        vmem_limit_bytes=VMEM_LIMIT_BYTES)


FFN_LAG = 2


def _ffn_kernel(n_pairs, n_chunks,
                x_ref, xres_ref, g_ref, wg_ref, wu_ref, wd_ref, o_ref,
                xn_ref, gate0_ref, gate1_ref, up0_ref, up1_ref, h0_ref, h1_ref, ss_ref, inv_ref):
    n = pl.program_id(0)
    down_chunk = (n - FFN_LAG) % n_chunks

    @pl.when(n == 0)
    def _():
        for ref in (gate0_ref, gate1_ref, up0_ref, up1_ref, h0_ref, h1_ref):
            ref[...] = jnp.zeros_like(ref)

    @pl.when((n % n_chunks == 0) & (n < n_pairs))
    def _():
        _prenorm(x_ref, g_ref, xn_ref, ss_ref, inv_ref)

    @pl.when((n == 0) | (down_chunk == 0))
    def _():
        o_ref[...] = jnp.zeros_like(o_ref)

    def stages(gate_w, up_w, gate_r, up_r, h_w, h_r):
        xn = xn_ref[...]
        n_rows = xn.shape[0]
        pieces = [(w_ref, dst, c0) for w_ref, dst in ((wg_ref, gate_w), (wu_ref, up_w))
                  for c0 in range(0, FF_TILE, MXU_COLS)]
        cuts = [p * n_rows // len(pieces) // NORM_ROWS * NORM_ROWS for p in range(len(pieces))]
        cuts.append(n_rows)
        for p, (w_ref, dst, c0) in enumerate(pieces):
            dst[:, c0:c0 + MXU_COLS] = _dot(xn, w_ref[:, c0:c0 + MXU_COLS])
            rows = slice(cuts[p], cuts[p + 1])
            g = gate_r[rows, :]
            h_w[rows, :] = (g * jax.nn.sigmoid(g) * up_r[rows, :]).astype(BF16)
        o_ref[...] += _dot(h_r[...], wd_ref[...])

    @pl.when(n % 2 == 0)
    def _():
        stages(gate0_ref, up0_ref, gate1_ref, up1_ref, h1_ref, h0_ref)

    @pl.when(n % 2 == 1)
    def _():
        stages(gate1_ref, up1_ref, gate0_ref, up0_ref, h0_ref, h1_ref)

    @pl.when((n >= FFN_LAG) & (down_chunk == n_chunks - 1))
    def _():
        _postnorm_residual(xres_ref, g_ref, o_ref, 0.5, ss_ref, inv_ref)


def _ffn(x, gains, wg, wu, wd, layer, row_tile):
    m, d = x.shape
    d_ff = wg.shape[-1]
    n_tiles, n_chunks = m // row_tile, d_ff // FF_TILE
    n_pairs = n_tiles * n_chunks

    def lagged_tile(n):
        return jnp.clip((n - FFN_LAG) // n_chunks, 0, n_tiles - 1)

    return pl.pallas_call(
        functools.partial(_ffn_kernel, n_pairs, n_chunks),
        grid=(n_pairs + FFN_LAG,),
        in_specs=[
            pl.BlockSpec((row_tile, d), lambda n: (jnp.minimum(n // n_chunks, n_tiles - 1), 0)),
            pl.BlockSpec((row_tile, d), lambda n: (lagged_tile(n), 0)),
            pl.BlockSpec((2, d), lambda n: (0, 0)),
            pl.BlockSpec((None, d, FF_TILE), lambda n: (layer, 0, n % n_chunks)),
            pl.BlockSpec((None, d, FF_TILE), lambda n: (layer, 0, n % n_chunks)),
            pl.BlockSpec((None, FF_TILE, d), lambda n: (layer, (n - FFN_LAG) % n_chunks, 0)),
        ],
        out_specs=pl.BlockSpec((row_tile, d), lambda n: (lagged_tile(n), 0)),
        out_shape=jax.ShapeDtypeStruct((m, d), F32),
        scratch_shapes=[
            pltpu.VMEM((row_tile, d), BF16),
            pltpu.VMEM((row_tile, FF_TILE), F32),
            pltpu.VMEM((row_tile, FF_TILE), F32),
            pltpu.VMEM((row_tile, FF_TILE), F32),
            pltpu.VMEM((row_tile, FF_TILE), F32),
            pltpu.VMEM((row_tile, FF_TILE), BF16),
            pltpu.VMEM((row_tile, FF_TILE), BF16),
            pltpu.VMEM((row_tile, LANES), F32),
            pltpu.VMEM((row_tile, LANES), F32),
        ],
        compiler_params=pltpu.CompilerParams(
            dimension_semantics=("arbitrary",), vmem_limit_bytes=VMEM_LIMIT_BYTES),
        name="ffn",
    )(x, x, gains, wg, wu, wd)


def _neg_softplus_neg(lam):
    return -(jnp.maximum(-lam, 0.0) + jnp.log1p(jnp.exp(-jnp.abs(lam))))


def _rglru_coeffs(r_pre, i_pre, u, log_sig_lam):
    r = jax.nn.sigmoid(r_pre)
    ig = jax.nn.sigmoid(i_pre)
    log_a = RG_C * r * log_sig_lam
    a = jnp.exp(log_a)
    one_minus_a2 = -jnp.tanh(log_a) * (a * a + 1.0)
    return a, jnp.sqrt(one_minus_a2) * (ig * u)


def _conv_taps(stage_ref, w_ref, r0, rows, width):
    hw = width - 1
    out = stage_ref[SUBLANES + r0:SUBLANES + r0 + rows, :] * w_ref[hw:hw + 1, :]
    for q in range(hw):
        lo = SUBLANES - hw + q + r0
        out = out + stage_ref[lo:lo + rows, :] * w_ref[q:q + 1, :]
    return out


def _rglru_main_kernel(x_ref, g_ref, wgate_ref, wu_ref, cw_ref, cb_ref, gw_ref, gb_ref,
                       lam_ref, wo_ref, h0_ref, hist0_ref,
                       o_ref, hlast_ref, histlast_ref,
                       xn_ref, state_ref, gate_ref, stage_ref, u_ref, gm_ref, y_ref,
                       ss_ref, inv_ref):
    i = pl.program_id(0)
    c = pl.program_id(1)
    n_rows = x_ref.shape[0]
    hw = CONV_A - 1
    n_blocks = CH_TILE // RG_BLOCK
    block_cols = [slice(nb * RG_BLOCK, (nb + 1) * RG_BLOCK) for nb in range(n_blocks)]

    @pl.when(c == 0)
    def _():
        _prenorm(x_ref, g_ref, xn_ref, ss_ref, inv_ref)
        o_ref[...] = jnp.zeros_like(o_ref)

    @pl.when(i == 0)
    def _():
        state_ref[c, 0:1, :] = h0_ref[0:1, :]
        state_ref[c, 1:1 + hw, :] = hist0_ref[0]

    xn = xn_ref[...]
    gate_ref[...] = _dot(xn, wgate_ref[...])
    stage_ref[SUBLANES:SUBLANES + n_rows, :] = _dot(xn, wu_ref[...])

    stage_ref[SUBLANES - hw:SUBLANES, :] = state_ref[c, 1:1 + hw, :]
    new_hist = stage_ref[SUBLANES + n_rows - hw:SUBLANES + n_rows, :]
    state_ref[c, 1:1 + hw, :] = new_hist
    histlast_ref[0] = new_hist
    for r0 in range(0, n_rows, CONV_ROWS):
        u_ref[r0:r0 + CONV_ROWS, :] = (
            _conv_taps(stage_ref, cw_ref, r0, CONV_ROWS, CONV_A) + cb_ref[...])

    for nb in range(n_blocks):
        ub = u_ref[:, block_cols[nb]].astype(BF16)
        gm_ref[:, 2 * nb * RG_BLOCK:2 * (nb + 1) * RG_BLOCK] = _dot(ub, gw_ref[nb])

    log_sig_lam = _neg_softplus_neg(lam_ref[...])
    sub = lax.broadcasted_iota(jnp.int32, (SUBLANES, RG_BLOCK), 0)

    def slab_scan(a, b, h_prev):
        for dist in (1, 2, 4):
            a_up = jnp.where(sub >= dist, pltpu.roll(a, dist, 0), 1.0)
            b_up = jnp.where(sub >= dist, pltpu.roll(b, dist, 0), 0.0)
            b = a * b_up + b
            a = a * a_up
        return b + a * h_prev

    def rec_chunk(k, h_prev):
        r0 = pl.multiple_of(k * NORM_ROWS, NORM_ROWS)
        rows = pl.ds(r0, NORM_ROWS)
        h_next = []
        for nb in range(n_blocks):
            cols = block_cols[nb]
            g0 = 2 * nb * RG_BLOCK
            bias = gb_ref[nb]
            a, b = _rglru_coeffs(
                gm_ref[rows, g0:g0 + RG_BLOCK] + bias[:, :RG_BLOCK],
                gm_ref[rows, g0 + RG_BLOCK:g0 + 2 * RG_BLOCK] + bias[:, RG_BLOCK:],
                u_ref[rows, cols], log_sig_lam[:, cols])
            h_lo = slab_scan(a[:SUBLANES], b[:SUBLANES], h_prev[nb])
            h_hi = slab_scan(a[SUBLANES:], b[SUBLANES:], h_lo[SUBLANES - 1:SUBLANES, :])
            h = jnp.concatenate([h_lo, h_hi], axis=0)
            y_ref[rows, cols] = (jax.nn.gelu(gate_ref[rows, cols]) * h).astype(BF16)
            h_next.append(h_hi[SUBLANES - 1:SUBLANES, :])
        return tuple(h_next)

    h_in = state_ref[c, 0:1, :]
    h_out = lax.fori_loop(0, n_rows // NORM_ROWS, rec_chunk,
                          tuple(h_in[:, cols] for cols in block_cols), unroll=2)
    h_out = jnp.concatenate(h_out, axis=-1)
    state_ref[c, 0:1, :] = h_out
    hlast_ref[0:1, :] = h_out

    o_ref[...] += _dot(y_ref[...], wo_ref[...])

    @pl.when(c == pl.num_programs(1) - 1)
    def _():
        _postnorm_residual(x_ref, g_ref, o_ref, 1.0, ss_ref, inv_ref)


def _rglru(x, gains, w_in, conv_w, conv_b, gate_w, gate_b, lam, w_out, h0, hist0,
           layer, row_tile, segments):
    m, d = x.shape
    d_rnn = w_out.shape[-2]
    nc = d_rnn // CH_TILE
    nb = CH_TILE // RG_BLOCK
    hw = CONV_A - 1
    n_seq = 1 if segments is None else len(segments)
    grid = (m // row_tile, nc)
    tile_f32 = pltpu.VMEM((row_tile, CH_TILE), F32)
    norm_scratch = [pltpu.VMEM((row_tile, LANES), F32)] * 2
    if segments is None:
        kernel = _rglru_main_kernel
        scratch = [
            pltpu.VMEM((row_tile, d), BF16),
            pltpu.VMEM((nc, SUBLANES, CH_TILE), F32),
            tile_f32,
            pltpu.VMEM((row_tile + SUBLANES, CH_TILE), F32),
            tile_f32,
            pltpu.VMEM((row_tile, 2 * CH_TILE), F32),
            pltpu.VMEM((row_tile, CH_TILE), BF16),
        ] + norm_scratch
    else:
        kernel = functools.partial(_rglru_small_kernel, tuple(segments))
        scratch = [
            pltpu.VMEM((row_tile, d), BF16),
            pltpu.VMEM((row_tile + SUBLANES, CH_TILE), F32),
            tile_f32, tile_f32, tile_f32, tile_f32,
        ] + norm_scratch
    return pl.pallas_call(
        kernel,
        grid=grid,
        in_specs=[
            pl.BlockSpec((row_tile, d), lambda i, c: (i, 0)),
            pl.BlockSpec((2, d), lambda i, c: (0, 0)),
            pl.BlockSpec((None, d, CH_TILE), lambda i, c: (layer, 0, c)),
            pl.BlockSpec((None, d, CH_TILE), lambda i, c: (layer, 0, nc + c)),
            pl.BlockSpec((None, CONV_A, CH_TILE), lambda i, c: (layer, 0, c)),
            pl.BlockSpec((None, 1, CH_TILE), lambda i, c: (layer, 0, c)),
            pl.BlockSpec((None, nb, RG_BLOCK, 2 * RG_BLOCK), lambda i, c: (layer, c, 0, 0)),
            pl.BlockSpec((None, nb, 1, 2 * RG_BLOCK), lambda i, c: (layer, c, 0, 0)),
            pl.BlockSpec((None, 1, CH_TILE), lambda i, c: (layer, 0, c)),
            pl.BlockSpec((None, CH_TILE, d), lambda i, c: (layer, c, 0)),
            pl.BlockSpec((n_seq, CH_TILE), lambda i, c: (0, c)),
            pl.BlockSpec((n_seq, hw, CH_TILE), lambda i, c: (0, 0, c)),
        ],
        out_specs=[
            pl.BlockSpec((row_tile, d), lambda i, c: (i, 0)),
            pl.BlockSpec((None, n_seq, CH_TILE), lambda i, c: (i, 0, c)),
            pl.BlockSpec((None, n_seq, hw, CH_TILE), lambda i, c: (i, 0, 0, c)),
        ],
        out_shape=[
            jax.ShapeDtypeStruct((m, d), F32),
            jax.ShapeDtypeStruct((grid[0], n_seq, d_rnn), F32),
            jax.ShapeDtypeStruct((grid[0], n_seq, hw, d_rnn), F32),
        ],
        scratch_shapes=scratch,
        compiler_params=_compiler_params(),
        name="rglru",
    )(x, gains, w_in, w_in, conv_w, conv_b, gate_w, gate_b, lam, w_out, h0, hist0)


def _segment_conv(stage_ref, seq, hist, w_ref, width):
    n = seq.shape[0]
    hw = width - 1
    stage_ref[SUBLANES - hw:SUBLANES, :] = hist
    stage_ref[SUBLANES:SUBLANES + n, :] = seq
    return (_conv_taps(stage_ref, w_ref, 0, n, width),
            stage_ref[SUBLANES + n - hw:SUBLANES + n, :])


def _rglru_small_kernel(segments,
                        x_ref, g_ref, wgate_ref, wu_ref, cw_ref, cb_ref, gw_ref, gb_ref,
                        lam_ref, wo_ref, h0_ref, hist0_ref,
                        o_ref, hlast_ref, histlast_ref,
                        xn_ref, stage_ref, u_ref, a_ref, b_ref, h_ref, ss_ref, inv_ref):
    c = pl.program_id(1)

    @pl.when(c == 0)
    def _():
        _prenorm(x_ref, g_ref, xn_ref, ss_ref, inv_ref)
        o_ref[...] = jnp.zeros_like(o_ref)

    xn = xn_ref[...]
    gate = _dot(xn, wgate_ref[...])
    u_pre = _dot(xn, wu_ref[...])

    for s, (r0, n) in enumerate(segments):
        conv, new_hist = _segment_conv(stage_ref, u_pre[r0:r0 + n, :], hist0_ref[s], cw_ref, CONV_A)
        u_ref[r0:r0 + n, :] = conv + cb_ref[...]
        histlast_ref[s] = new_hist

    log_sig_lam = _neg_softplus_neg(lam_ref[...])
    for nb in range(CH_TILE // RG_BLOCK):
        cols = slice(nb * RG_BLOCK, (nb + 1) * RG_BLOCK)
        ub = u_ref[:, cols]
        gm = _dot(ub.astype(BF16), gw_ref[nb]) + gb_ref[nb]
        a, b = _rglru_coeffs(gm[:, :RG_BLOCK], gm[:, RG_BLOCK:], ub, log_sig_lam[:, cols])
        a_ref[:, cols] = a
        b_ref[:, cols] = b

    for s, (r0, n) in enumerate(segments):
        def step(t, h):
            h = a_ref[pl.ds(t, 1), :] * h + b_ref[pl.ds(t, 1), :]
            h_ref[pl.ds(t, 1), :] = h
            return h
        hlast_ref[s:s + 1, :] = lax.fori_loop(r0, r0 + n, step, h0_ref[s:s + 1, :])

    y = (jax.nn.gelu(gate) * h_ref[...]).astype(BF16)
    o_ref[...] += _dot(y, wo_ref[...])

    @pl.when(c == pl.num_programs(1) - 1)
    def _():
        _postnorm_residual(x_ref, g_ref, o_ref, 1.0, ss_ref, inv_ref)


def _shortconv_small_kernel(segments,
                            x_ref, g_ref, wb_ref, wc_ref, wv_ref, cw_ref, wo_ref, hist0_ref,
                            o_ref, histlast_ref,
                            xn_ref, stage_ref, y_ref, ss_ref, inv_ref):
    c = pl.program_id(1)

    @pl.when(c == 0)
    def _():
        _prenorm(x_ref, g_ref, xn_ref, ss_ref, inv_ref)
        o_ref[...] = jnp.zeros_like(o_ref)

    xn = xn_ref[...]
    gb = _dot(xn, wb_ref[...])
    cv = _dot(xn, wc_ref[...]) * _dot(xn, wv_ref[...])

    for s, (r0, n) in enumerate(segments):
        z, new_hist = _segment_conv(stage_ref, cv[r0:r0 + n, :], hist0_ref[s], cw_ref, CONV_B)
        y_ref[r0:r0 + n, :] = (gb[r0:r0 + n, :] * z).astype(BF16)
        histlast_ref[s] = new_hist

    o_ref[...] += _dot(y_ref[...], wo_ref[...])

    @pl.when(c == pl.num_programs(1) - 1)
    def _():
        _postnorm_residual(x_ref, g_ref, o_ref, 1.0, ss_ref, inv_ref)


def _shortconv_main_kernel(x_ref, g_ref, wb_ref, wc_ref, wv_ref, cw_ref, wo_ref, hist0_ref,
                           o_ref, histlast_ref,
                           xn_ref, state_ref, b_ref, c_ref, stage_ref, y_ref, ss_ref, inv_ref):
    i = pl.program_id(0)
    c = pl.program_id(1)
    n_rows = x_ref.shape[0]
    hw = CONV_B - 1

    @pl.when(c == 0)
    def _():
        _prenorm(x_ref, g_ref, xn_ref, ss_ref, inv_ref)
        o_ref[...] = jnp.zeros_like(o_ref)

    @pl.when(i == 0)
    def _():
        state_ref[c, 0:hw, :] = hist0_ref[0]

    xn = xn_ref[...]
    b_ref[...] = _dot(xn, wb_ref[...])
    c_ref[...] = _dot(xn, wc_ref[...])
    stage_ref[SUBLANES:SUBLANES + n_rows, :] = _dot(xn, wv_ref[...])

    def cv_chunk(r0):
        rows = pl.ds(pl.multiple_of(SUBLANES + r0, SUBLANES), NORM_ROWS)
        stage_ref[rows, :] = stage_ref[rows, :] * c_ref[pl.ds(r0, NORM_ROWS), :]
    _row_chunks(n_rows, NORM_ROWS, cv_chunk)

    stage_ref[SUBLANES - hw:SUBLANES, :] = state_ref[c, 0:hw, :]
    new_hist = stage_ref[SUBLANES + n_rows - hw:SUBLANES + n_rows, :]
    state_ref[c, 0:hw, :] = new_hist
    histlast_ref[0] = new_hist

    for r0 in range(0, n_rows, CONV_ROWS):
        z = _conv_taps(stage_ref, cw_ref, r0, CONV_ROWS, CONV_B)
        y_ref[r0:r0 + CONV_ROWS, :] = (b_ref[r0:r0 + CONV_ROWS, :] * z).astype(BF16)

    o_ref[...] += _dot(y_ref[...], wo_ref[...])

    @pl.when(c == pl.num_programs(1) - 1)
    def _():
        _postnorm_residual(x_ref, g_ref, o_ref, 1.0, ss_ref, inv_ref)


def _shortconv(x, gains, w_in, conv_w, w_out, hist0, layer, row_tile, segments):
    m, d = x.shape
    d_conv = w_out.shape[-2]
    nc = d_conv // CH_TILE
    hw = CONV_B - 1
    n_seq = 1 if segments is None else len(segments)
    grid = (m // row_tile, nc)
    tile_f32 = pltpu.VMEM((row_tile, CH_TILE), F32)
    norm_scratch = [pltpu.VMEM((row_tile, LANES), F32)] * 2
    if segments is None:
        kernel = _shortconv_main_kernel
        scratch = [
            pltpu.VMEM((row_tile, d), BF16),
            pltpu.VMEM((nc, SUBLANES, CH_TILE), F32),
            tile_f32, tile_f32,
            pltpu.VMEM((row_tile + SUBLANES, CH_TILE), F32),
            pltpu.VMEM((row_tile, CH_TILE), BF16),
        ] + norm_scratch
    else:
        kernel = functools.partial(_shortconv_small_kernel, tuple(segments))
        scratch = [
            pltpu.VMEM((row_tile, d), BF16),
            pltpu.VMEM((row_tile + SUBLANES, CH_TILE), F32),
            pltpu.VMEM((row_tile, CH_TILE), BF16),
        ] + norm_scratch
    return pl.pallas_call(
        kernel,
        grid=grid,
        in_specs=[
            pl.BlockSpec((row_tile, d), lambda i, c: (i, 0)),
            pl.BlockSpec((2, d), lambda i, c: (0, 0)),
            pl.BlockSpec((None, d, CH_TILE), lambda i, c: (layer, 0, c)),
            pl.BlockSpec((None, d, CH_TILE), lambda i, c: (layer, 0, nc + c)),
            pl.BlockSpec((None, d, CH_TILE), lambda i, c: (layer, 0, 2 * nc + c)),
            pl.BlockSpec((None, CONV_B, CH_TILE), lambda i, c: (layer, 0, c)),
            pl.BlockSpec((None, CH_TILE, d), lambda i, c: (layer, c, 0)),
            pl.BlockSpec((n_seq, hw, CH_TILE), lambda i, c: (0, 0, c)),
        ],
        out_specs=[
            pl.BlockSpec((row_tile, d), lambda i, c: (i, 0)),
            pl.BlockSpec((None, n_seq, hw, CH_TILE), lambda i, c: (i, 0, 0, c)),
        ],
        out_shape=[
            jax.ShapeDtypeStruct((m, d), F32),
            jax.ShapeDtypeStruct((grid[0], n_seq, hw, d_conv), F32),
        ],
        scratch_shapes=scratch,
        compiler_params=_compiler_params(),
        name="shortconv",
    )(x, gains, w_in, w_in, w_in, conv_w, w_out, hist0)


def kernel(x_prompt, x_sample, state_a_h, cache_a_conv, cache_b_conv, meta_tokens, norm_g, ffn1_wg, ffn1_wu, ffn1_wd, ffn2_wg, ffn2_wu, ffn2_wd, a_w_in, a_conv_w, a_conv_b, a_gate_w, a_gate_b, a_lambda, a_w_out, b_w_in, b_conv_w, b_w_out):
    depth = norm_g.shape[0]
    n_prompt, seq, d = x_prompt.shape
    n_dec, dec_seq, _ = x_sample.shape
    assert n_prompt == 1, "prompt rows are treated as one causal sequence"
    assert seq % ROW_TILE == 0

    x_main = x_prompt.reshape(seq, d)
    x_small = jnp.concatenate(
        [meta_tokens.astype(F32), x_sample.reshape(n_dec * dec_seq, d)], axis=0)
    m_small = x_small.shape[0]
    small_segments = [(0, N_META)] + [(N_META + b * dec_seq, dec_seq) for b in range(n_dec)]

    bf = lambda w: w.astype(BF16)
    ffn_w = [(bf(ffn1_wg), bf(ffn1_wu), bf(ffn1_wd)), (bf(ffn2_wg), bf(ffn2_wu), bf(ffn2_wd))]
    a_w_in_b, a_w_out_b, a_gate_w_b = bf(a_w_in), bf(a_w_out), bf(a_gate_w)
    b_w_in_b, b_w_out_b = bf(b_w_in), bf(b_w_out)
    a_conv_b3 = a_conv_b[:, None, :]
    a_lambda3 = a_lambda[:, None, :]
    a_gate_b4 = a_gate_b[:, :, None, :]

    p_h, p_ha, p_hb, s_h, s_ha, s_hb = [], [], [], [], [], []
    for layer in range(depth):
        g = norm_g[layer]
        j = layer // 2

        x_main = _ffn(x_main, g[0:2], *ffn_w[0], layer, ROW_TILE)
        x_small = _ffn(x_small, g[0:2], *ffn_w[0], layer, m_small)

        if layer % 2 == 0:
            h0 = jnp.concatenate([jnp.zeros((1, state_a_h.shape[-1]), F32), state_a_h[j]], axis=0)
            hist0 = jnp.concatenate(
                [jnp.zeros((1,) + cache_a_conv.shape[2:], F32), cache_a_conv[j]], axis=0)
            a_args = (a_w_in_b, a_conv_w, a_conv_b3, a_gate_w_b, a_gate_b4, a_lambda3, a_w_out_b)
            x_small, h_s, hist_s = _rglru(x_small, g[2:4], *a_args, h0, hist0,
                                          j, m_small, small_segments)
            h_s, hist_s = h_s[0], hist_s[0]
            x_main, h_m, hist_m = _rglru(x_main, g[2:4], *a_args, h_s[0:1], hist_s[0:1],
                                         j, ROW_TILE, None)
            p_h.append(h_m[-1])
            p_ha.append(hist_m[-1])
            s_h.append(h_s[1:])
            s_ha.append(hist_s[1:])
        else:
            hist0 = jnp.concatenate(
                [jnp.zeros((1,) + cache_b_conv.shape[2:], F32), cache_b_conv[j]], axis=0)
            b_args = (b_w_in_b, b_conv_w, b_w_out_b)
            x_small, hist_s = _shortconv(x_small, g[2:4], *b_args, hist0,
                                         j, m_small, small_segments)
            hist_s = hist_s[0]
            x_main, hist_m = _shortconv(x_main, g[2:4], *b_args, hist_s[0:1],
                                        j, ROW_TILE, None)
            p_hb.append(hist_m[-1])
            s_hb.append(hist_s[1:])

        x_main = _ffn(x_main, g[4:6], *ffn_w[1], layer, ROW_TILE)
        x_small = _ffn(x_small, g[4:6], *ffn_w[1], layer, m_small)

    y_prompt = x_main.reshape(1, seq, d)
    y_sample = x_small[N_META:].reshape(n_dec, dec_seq, d)
    return (y_prompt, y_sample,
            jnp.stack(p_h), jnp.stack(p_ha), jnp.stack(p_hb),
            jnp.stack(s_h), jnp.stack(s_ha), jnp.stack(s_hb))
```

```python
import functools

import jax
import jax.numpy as jnp
from jax import lax
from jax.experimental import pallas as pl
from jax.experimental.pallas import tpu as pltpu

F32 = jnp.float32
BF16 = jnp.bfloat16

EPS = 1e-6
RG_C = 8.0
RG_BLOCK = 128
N_META = 16
CONV_A = 4
CONV_B = 3

LANES = 128
SUBLANES = 8
MXU_COLS = 256
VMEM_LIMIT_BYTES = 56 * 1024 * 1024

ROW_TILE = 512
FF_TILE = 512
CH_TILE = 512
NORM_ROWS = 16
CONV_ROWS = 64
LOOP_UNROLL = 4


def _dot(a, b):
    return jnp.dot(a, b, preferred_element_type=F32)


def _row_chunks(n_rows, rows_per_chunk, body, unroll=LOOP_UNROLL):
    def step(k, _):
        body(pl.multiple_of(k * rows_per_chunk, rows_per_chunk))
        return 0
    lax.fori_loop(0, n_rows // rows_per_chunk, step, 0, unroll=unroll)


def _lane_tile(v, d):
    return jnp.concatenate([v] * (d // LANES), axis=-1)


def _inv_rms_to(src_ref, scale, ss_ref, inv_ref):
    n_rows, d = src_ref.shape

    def partial(r0):
        x = src_ref[pl.ds(r0, SUBLANES), :]
        x2 = x * x
        acc = x2[:, 0:LANES]
        for q in range(1, d // LANES):
            acc = acc + x2[:, q * LANES:(q + 1) * LANES]
        ss_ref[pl.ds(r0, SUBLANES), :] = acc
    _row_chunks(n_rows, SUBLANES, partial)
    tot = jnp.sum(ss_ref[...], axis=-1, keepdims=True)
    inv = scale * lax.rsqrt(tot * (1.0 / d) + EPS)
    inv_ref[...] = jnp.broadcast_to(inv, inv_ref.shape)


def _prenorm(x_ref, g_ref, xn_ref, ss_ref, inv_ref):
    n_rows, d = x_ref.shape
    _inv_rms_to(x_ref, 1.0, ss_ref, inv_ref)
    gain = jnp.broadcast_to(g_ref[0:1, :], (SUBLANES, d))
    gain = jnp.concatenate([gain] * (NORM_ROWS // SUBLANES), axis=0)

    def apply(r0):
        rows = pl.ds(r0, NORM_ROWS)
        inv = _lane_tile(inv_ref[rows, :], d)
        xn_ref[rows, :] = ((x_ref[rows, :] * inv) * gain).astype(BF16)
    _row_chunks(n_rows, NORM_ROWS, apply)


def _postnorm_residual(x_ref, g_ref, o_ref, scale, ss_ref, inv_ref):
    n_rows, d = x_ref.shape
    _inv_rms_to(o_ref, scale, ss_ref, inv_ref)
    gain = jnp.broadcast_to(g_ref[1:2, :], (SUBLANES, d))

    def apply(r0):
        rows = pl.ds(r0, SUBLANES)
        inv = _lane_tile(inv_ref[rows, :], d)
        o_ref[rows, :] = x_ref[rows, :] + (o_ref[rows, :] * inv) * gain
    _row_chunks(n_rows, SUBLANES, apply)


def _compiler_params():
    return pltpu.CompilerParams(
        dimension_semantics=("arbitrary", "arbitrary"),
        vmem_limit_bytes=VMEM_LIMIT_BYTES)


def _ffn_kernel(x_ref, g_ref, wg_ref, wu_ref, wd_ref, o_ref,
                xn_ref, gate_ref, up_ref, h_ref, ss_ref, inv_ref):
    j = pl.program_id(1)

    @pl.when(j == 0)
    def _():
        _prenorm(x_ref, g_ref, xn_ref, ss_ref, inv_ref)
        o_ref[...] = jnp.zeros_like(o_ref)

    xn = xn_ref[...]
    gate_ref[...] = _dot(xn, wg_ref[...])
    up_ref[...] = _dot(xn, wu_ref[...])
    g = gate_ref[...]
    h_ref[...] = (g * jax.nn.sigmoid(g) * up_ref[...]).astype(BF16)
    o_ref[...] += _dot(h_ref[...], wd_ref[...])

    @pl.when(j == pl.num_programs(1) - 1)
    def _():
        _postnorm_residual(x_ref, g_ref, o_ref, 0.5, ss_ref, inv_ref)


def _ffn(x, gains, wg, wu, wd, layer, row_tile):
    m, d = x.shape
    n_chunks = wg.shape[1]
    grid = (m // row_tile, n_chunks)
    return pl.pallas_call(
        _ffn_kernel,
        grid=grid,
        in_specs=[
            pl.BlockSpec((row_tile, d), lambda i, j: (i, 0)),
            pl.BlockSpec((2, d), lambda i, j: (0, 0)),
            pl.BlockSpec((None, None, d, FF_TILE), lambda i, j: (layer, j, 0, 0)),
            pl.BlockSpec((None, None, d, FF_TILE), lambda i, j: (layer, j, 0, 0)),
            pl.BlockSpec((None, FF_TILE, d), lambda i, j: (layer, j, 0)),
        ],
        out_specs=pl.BlockSpec((row_tile, d), lambda i, j: (i, 0)),
        out_shape=jax.ShapeDtypeStruct((m, d), F32),
        scratch_shapes=[
            pltpu.VMEM((row_tile, d), BF16),
            pltpu.VMEM((row_tile, FF_TILE), F32),
            pltpu.VMEM((row_tile, FF_TILE), F32),
            pltpu.VMEM((row_tile, FF_TILE), BF16),
            pltpu.VMEM((row_tile, LANES), F32),
            pltpu.VMEM((row_tile, LANES), F32),
        ],
        compiler_params=_compiler_params(),
        name="ffn",
    )(x, gains, wg, wu, wd)


def _neg_softplus_neg(lam):
    return -(jnp.maximum(-lam, 0.0) + jnp.log1p(jnp.exp(-jnp.abs(lam))))


def _rglru_coeffs(r_pre, i_pre, u, log_sig_lam):
    r = jax.nn.sigmoid(r_pre)
    ig = jax.nn.sigmoid(i_pre)
    log_a = RG_C * r * log_sig_lam
    a = jnp.exp(log_a)
    one_minus_a2 = -jnp.tanh(log_a) * (a * a + 1.0)
    return a, jnp.sqrt(one_minus_a2) * (ig * u)


def _conv_taps(stage_ref, w_ref, r0, rows, width):
    hw = width - 1
    out = stage_ref[SUBLANES + r0:SUBLANES + r0 + rows, :] * w_ref[hw:hw + 1, :]
    for q in range(hw):
        lo = SUBLANES - hw + q + r0
        out = out + stage_ref[lo:lo + rows, :] * w_ref[q:q + 1, :]
    return out


def _rglru_main_kernel(x_ref, g_ref, wgate_ref, wu_ref, cw_ref, cb_ref, gw_ref, gb_ref,
                       lam_ref, wo_ref, h0_ref, hist0_ref,
                       o_ref, hlast_ref, histlast_ref,
                       xn_ref, state_ref, gate_ref, stage_ref, u_ref, gm_ref, y_ref,
                       ss_ref, inv_ref):
    i = pl.program_id(0)
    c = pl.program_id(1)
    n_rows = x_ref.shape[0]
    hw = CONV_A - 1
    n_blocks = CH_TILE // RG_BLOCK
    block_cols = [slice(nb * RG_BLOCK, (nb + 1) * RG_BLOCK) for nb in range(n_blocks)]

    @pl.when(c == 0)
    def _():
        _prenorm(x_ref, g_ref, xn_ref, ss_ref, inv_ref)
        o_ref[...] = jnp.zeros_like(o_ref)

    @pl.when(i == 0)
    def _():
        state_ref[c, 0:1, :] = h0_ref[0:1, :]
        state_ref[c, 1:1 + hw, :] = hist0_ref[0]

    xn = xn_ref[...]
    gate_ref[...] = _dot(xn, wgate_ref[...])
    stage_ref[SUBLANES:SUBLANES + n_rows, :] = _dot(xn, wu_ref[...])

    stage_ref[SUBLANES - hw:SUBLANES, :] = state_ref[c, 1:1 + hw, :]
    new_hist = stage_ref[SUBLANES + n_rows - hw:SUBLANES + n_rows, :]
    state_ref[c, 1:1 + hw, :] = new_hist
    histlast_ref[0] = new_hist
    for r0 in range(0, n_rows, CONV_ROWS):
        u_ref[r0:r0 + CONV_ROWS, :] = (
            _conv_taps(stage_ref, cw_ref, r0, CONV_ROWS, CONV_A) + cb_ref[...])

    for nb in range(n_blocks):
        ub = u_ref[:, block_cols[nb]].astype(BF16)
        gm_ref[:, 2 * nb * RG_BLOCK:2 * (nb + 1) * RG_BLOCK] = _dot(ub, gw_ref[nb])

    log_sig_lam = _neg_softplus_neg(lam_ref[...])
    sub = lax.broadcasted_iota(jnp.int32, (SUBLANES, RG_BLOCK), 0)

    def slab_scan(a, b, h_prev):
        for dist in (1, 2, 4):
            a_up = jnp.where(sub >= dist, pltpu.roll(a, dist, 0), 1.0)
            b_up = jnp.where(sub >= dist, pltpu.roll(b, dist, 0), 0.0)
            b = a * b_up + b
            a = a * a_up
        return b + a * h_prev

    def rec_chunk(k, h_prev):
        r0 = pl.multiple_of(k * NORM_ROWS, NORM_ROWS)
        rows = pl.ds(r0, NORM_ROWS)
        h_next = []
        for nb in range(n_blocks):
            cols = block_cols[nb]
            g0 = 2 * nb * RG_BLOCK
            bias = gb_ref[nb]
            a, b = _rglru_coeffs(
                gm_ref[rows, g0:g0 + RG_BLOCK] + bias[:, :RG_BLOCK],
                gm_ref[rows, g0 + RG_BLOCK:g0 + 2 * RG_BLOCK] + bias[:, RG_BLOCK:],
                u_ref[rows, cols], log_sig_lam[:, cols])
            h_lo = slab_scan(a[:SUBLANES], b[:SUBLANES], h_prev[nb])
            h_hi = slab_scan(a[SUBLANES:], b[SUBLANES:], h_lo[SUBLANES - 1:SUBLANES, :])
            h = jnp.concatenate([h_lo, h_hi], axis=0)
            y_ref[rows, cols] = (jax.nn.gelu(gate_ref[rows, cols]) * h).astype(BF16)
            h_next.append(h_hi[SUBLANES - 1:SUBLANES, :])
        return tuple(h_next)

    h_in = state_ref[c, 0:1, :]
    h_out = lax.fori_loop(0, n_rows // NORM_ROWS, rec_chunk,
                          tuple(h_in[:, cols] for cols in block_cols), unroll=2)
    h_out = jnp.concatenate(h_out, axis=-1)
    state_ref[c, 0:1, :] = h_out
    hlast_ref[0:1, :] = h_out

    o_ref[...] += _dot(y_ref[...], wo_ref[...])

    @pl.when(c == pl.num_programs(1) - 1)
    def _():
        _postnorm_residual(x_ref, g_ref, o_ref, 1.0, ss_ref, inv_ref)


def _rglru(x, gains, w_in, conv_w, conv_b, gate_w, gate_b, lam, w_out, h0, hist0,
           layer, row_tile, segments):
    m, d = x.shape
    d_rnn = w_out.shape[-2]
    nc = d_rnn // CH_TILE
    nb = CH_TILE // RG_BLOCK
    hw = CONV_A - 1
    n_seq = 1 if segments is None else len(segments)
    grid = (m // row_tile, nc)
    tile_f32 = pltpu.VMEM((row_tile, CH_TILE), F32)
    norm_scratch = [pltpu.VMEM((row_tile, LANES), F32)] * 2
    if segments is None:
        kernel = _rglru_main_kernel
        scratch = [
            pltpu.VMEM((row_tile, d), BF16),
            pltpu.VMEM((nc, SUBLANES, CH_TILE), F32),
            tile_f32,
            pltpu.VMEM((row_tile + SUBLANES, CH_TILE), F32),
            tile_f32,
            pltpu.VMEM((row_tile, 2 * CH_TILE), F32),
            pltpu.VMEM((row_tile, CH_TILE), BF16),
        ] + norm_scratch
    else:
        kernel = functools.partial(_rglru_small_kernel, tuple(segments))
        scratch = [
            pltpu.VMEM((row_tile, d), BF16),
            pltpu.VMEM((row_tile + SUBLANES, CH_TILE), F32),
            tile_f32, tile_f32, tile_f32, tile_f32,
        ] + norm_scratch
    return pl.pallas_call(
        kernel,
        grid=grid,
        in_specs=[
            pl.BlockSpec((row_tile, d), lambda i, c: (i, 0)),
            pl.BlockSpec((2, d), lambda i, c: (0, 0)),
            pl.BlockSpec((None, None, d, CH_TILE), lambda i, c: (layer, c, 0, 0)),
            pl.BlockSpec((None, None, d, CH_TILE), lambda i, c: (layer, nc + c, 0, 0)),
            pl.BlockSpec((None, CONV_A, CH_TILE), lambda i, c: (layer, 0, c)),
            pl.BlockSpec((None, 1, CH_TILE), lambda i, c: (layer, 0, c)),
            pl.BlockSpec((None, nb, RG_BLOCK, 2 * RG_BLOCK), lambda i, c: (layer, c, 0, 0)),
            pl.BlockSpec((None, nb, 1, 2 * RG_BLOCK), lambda i, c: (layer, c, 0, 0)),
            pl.BlockSpec((None, 1, CH_TILE), lambda i, c: (layer, 0, c)),
            pl.BlockSpec((None, CH_TILE, d), lambda i, c: (layer, c, 0)),
            pl.BlockSpec((n_seq, CH_TILE), lambda i, c: (0, c)),
            pl.BlockSpec((n_seq, hw, CH_TILE), lambda i, c: (0, 0, c)),
        ],
        out_specs=[
            pl.BlockSpec((row_tile, d), lambda i, c: (i, 0)),
            pl.BlockSpec((None, n_seq, CH_TILE), lambda i, c: (i, 0, c)),
            pl.BlockSpec((None, n_seq, hw, CH_TILE), lambda i, c: (i, 0, 0, c)),
        ],
        out_shape=[
            jax.ShapeDtypeStruct((m, d), F32),
            jax.ShapeDtypeStruct((grid[0], n_seq, d_rnn), F32),
            jax.ShapeDtypeStruct((grid[0], n_seq, hw, d_rnn), F32),
        ],
        scratch_shapes=scratch,
        compiler_params=_compiler_params(),
        name="rglru",
    )(x, gains, w_in, w_in, conv_w, conv_b, gate_w, gate_b, lam, w_out, h0, hist0)


def _segment_conv(stage_ref, seq, hist, w_ref, width):
    n = seq.shape[0]
    hw = width - 1
    stage_ref[SUBLANES - hw:SUBLANES, :] = hist
    stage_ref[SUBLANES:SUBLANES + n, :] = seq
    return (_conv_taps(stage_ref, w_ref, 0, n, width),
            stage_ref[SUBLANES + n - hw:SUBLANES + n, :])


def _rglru_small_kernel(segments,
                        x_ref, g_ref, wgate_ref, wu_ref, cw_ref, cb_ref, gw_ref, gb_ref,
                        lam_ref, wo_ref, h0_ref, hist0_ref,
                        o_ref, hlast_ref, histlast_ref,
                        xn_ref, stage_ref, u_ref, a_ref, b_ref, h_ref, ss_ref, inv_ref):
    c = pl.program_id(1)

    @pl.when(c == 0)
    def _():
        _prenorm(x_ref, g_ref, xn_ref, ss_ref, inv_ref)
        o_ref[...] = jnp.zeros_like(o_ref)

    xn = xn_ref[...]
    gate = _dot(xn, wgate_ref[...])
    u_pre = _dot(xn, wu_ref[...])

    for s, (r0, n) in enumerate(segments):
        conv, new_hist = _segment_conv(stage_ref, u_pre[r0:r0 + n, :], hist0_ref[s], cw_ref, CONV_A)
        u_ref[r0:r0 + n, :] = conv + cb_ref[...]
        histlast_ref[s] = new_hist

    log_sig_lam = _neg_softplus_neg(lam_ref[...])
    for nb in range(CH_TILE // RG_BLOCK):
        cols = slice(nb * RG_BLOCK, (nb + 1) * RG_BLOCK)
        ub = u_ref[:, cols]
        gm = _dot(ub.astype(BF16), gw_ref[nb]) + gb_ref[nb]
        a, b = _rglru_coeffs(gm[:, :RG_BLOCK], gm[:, RG_BLOCK:], ub, log_sig_lam[:, cols])
        a_ref[:, cols] = a
        b_ref[:, cols] = b

    for s, (r0, n) in enumerate(segments):
        def step(t, h):
            h = a_ref[pl.ds(t, 1), :] * h + b_ref[pl.ds(t, 1), :]
            h_ref[pl.ds(t, 1), :] = h
            return h
        hlast_ref[s:s + 1, :] = lax.fori_loop(r0, r0 + n, step, h0_ref[s:s + 1, :])

    y = (jax.nn.gelu(gate) * h_ref[...]).astype(BF16)
    o_ref[...] += _dot(y, wo_ref[...])

    @pl.when(c == pl.num_programs(1) - 1)
    def _():
        _postnorm_residual(x_ref, g_ref, o_ref, 1.0, ss_ref, inv_ref)


def _shortconv_small_kernel(segments,
                            x_ref, g_ref, wb_ref, wc_ref, wv_ref, cw_ref, wo_ref, hist0_ref,
                            o_ref, histlast_ref,
                            xn_ref, stage_ref, y_ref, ss_ref, inv_ref):
    c = pl.program_id(1)

    @pl.when(c == 0)
    def _():
        _prenorm(x_ref, g_ref, xn_ref, ss_ref, inv_ref)
        o_ref[...] = jnp.zeros_like(o_ref)

    xn = xn_ref[...]
    gb = _dot(xn, wb_ref[...])
    cv = _dot(xn, wc_ref[...]) * _dot(xn, wv_ref[...])

    for s, (r0, n) in enumerate(segments):
        z, new_hist = _segment_conv(stage_ref, cv[r0:r0 + n, :], hist0_ref[s], cw_ref, CONV_B)
        y_ref[r0:r0 + n, :] = (gb[r0:r0 + n, :] * z).astype(BF16)
        histlast_ref[s] = new_hist

    o_ref[...] += _dot(y_ref[...], wo_ref[...])

    @pl.when(c == pl.num_programs(1) - 1)
    def _():
        _postnorm_residual(x_ref, g_ref, o_ref, 1.0, ss_ref, inv_ref)


def _shortconv_main_kernel(x_ref, g_ref, wb_ref, wc_ref, wv_ref, cw_ref, wo_ref, hist0_ref,
                           o_ref, histlast_ref,
                           xn_ref, state_ref, b_ref, c_ref, stage_ref, y_ref, ss_ref, inv_ref):
    i = pl.program_id(0)
    c = pl.program_id(1)
    n_rows = x_ref.shape[0]
    hw = CONV_B - 1

    @pl.when(c == 0)
    def _():
        _prenorm(x_ref, g_ref, xn_ref, ss_ref, inv_ref)
        o_ref[...] = jnp.zeros_like(o_ref)

    @pl.when(i == 0)
    def _():
        state_ref[c, 0:hw, :] = hist0_ref[0]

    xn = xn_ref[...]
    b_ref[...] = _dot(xn, wb_ref[...])
    c_ref[...] = _dot(xn, wc_ref[...])
    stage_ref[SUBLANES:SUBLANES + n_rows, :] = _dot(xn, wv_ref[...])

    def cv_chunk(r0):
        rows = pl.ds(pl.multiple_of(SUBLANES + r0, SUBLANES), NORM_ROWS)
        stage_ref[rows, :] = stage_ref[rows, :] * c_ref[pl.ds(r0, NORM_ROWS), :]
    _row_chunks(n_rows, NORM_ROWS, cv_chunk)

    stage_ref[SUBLANES - hw:SUBLANES, :] = state_ref[c, 0:hw, :]
    new_hist = stage_ref[SUBLANES + n_rows - hw:SUBLANES + n_rows, :]
    state_ref[c, 0:hw, :] = new_hist
    histlast_ref[0] = new_hist

    for r0 in range(0, n_rows, CONV_ROWS):
        z = _conv_taps(stage_ref, cw_ref, r0, CONV_ROWS, CONV_B)
        y_ref[r0:r0 + CONV_ROWS, :] = (b_ref[r0:r0 + CONV_ROWS, :] * z).astype(BF16)

    o_ref[...] += _dot(y_ref[...], wo_ref[...])

    @pl.when(c == pl.num_programs(1) - 1)
    def _():
        _postnorm_residual(x_ref, g_ref, o_ref, 1.0, ss_ref, inv_ref)


def _shortconv(x, gains, w_in, conv_w, w_out, hist0, layer, row_tile, segments):
    m, d = x.shape
    d_conv = w_out.shape[-2]
    nc = d_conv // CH_TILE
    hw = CONV_B - 1
    n_seq = 1 if segments is None else len(segments)
    grid = (m // row_tile, nc)
    tile_f32 = pltpu.VMEM((row_tile, CH_TILE), F32)
    norm_scratch = [pltpu.VMEM((row_tile, LANES), F32)] * 2
    if segments is None:
        kernel = _shortconv_main_kernel
        scratch = [
            pltpu.VMEM((row_tile, d), BF16),
            pltpu.VMEM((nc, SUBLANES, CH_TILE), F32),
            tile_f32, tile_f32,
            pltpu.VMEM((row_tile + SUBLANES, CH_TILE), F32),
            pltpu.VMEM((row_tile, CH_TILE), BF16),
        ] + norm_scratch
    else:
        kernel = functools.partial(_shortconv_small_kernel, tuple(segments))
        scratch = [
            pltpu.VMEM((row_tile, d), BF16),
            pltpu.VMEM((row_tile + SUBLANES, CH_TILE), F32),
            pltpu.VMEM((row_tile, CH_TILE), BF16),
        ] + norm_scratch
    return pl.pallas_call(
        kernel,
        grid=grid,
        in_specs=[
            pl.BlockSpec((row_tile, d), lambda i, c: (i, 0)),
            pl.BlockSpec((2, d), lambda i, c: (0, 0)),
            pl.BlockSpec((None, None, d, CH_TILE), lambda i, c: (layer, c, 0, 0)),
            pl.BlockSpec((None, None, d, CH_TILE), lambda i, c: (layer, nc + c, 0, 0)),
            pl.BlockSpec((None, None, d, CH_TILE), lambda i, c: (layer, 2 * nc + c, 0, 0)),
            pl.BlockSpec((None, CONV_B, CH_TILE), lambda i, c: (layer, 0, c)),
            pl.BlockSpec((None, CH_TILE, d), lambda i, c: (layer, c, 0)),
            pl.BlockSpec((n_seq, hw, CH_TILE), lambda i, c: (0, 0, c)),
        ],
        out_specs=[
            pl.BlockSpec((row_tile, d), lambda i, c: (i, 0)),
            pl.BlockSpec((None, n_seq, hw, CH_TILE), lambda i, c: (i, 0, 0, c)),
        ],
        out_shape=[
            jax.ShapeDtypeStruct((m, d), F32),
            jax.ShapeDtypeStruct((grid[0], n_seq, hw, d_conv), F32),
        ],
        scratch_shapes=scratch,
        compiler_params=_compiler_params(),
        name="shortconv",
    )(x, gains, w_in, w_in, w_in, conv_w, w_out, hist0)


def kernel(x_prompt, x_sample, state_a_h, cache_a_conv, cache_b_conv, meta_tokens, norm_g, ffn1_wg, ffn1_wu, ffn1_wd, ffn2_wg, ffn2_wu, ffn2_wd, a_w_in, a_conv_w, a_conv_b, a_gate_w, a_gate_b, a_lambda, a_w_out, b_w_in, b_conv_w, b_w_out):
    depth = norm_g.shape[0]
    n_prompt, seq, d = x_prompt.shape
    n_dec, dec_seq, _ = x_sample.shape
    assert n_prompt == 1, "prompt rows are treated as one causal sequence"
    assert seq % ROW_TILE == 0

    x_main = x_prompt.reshape(seq, d)
    x_small = jnp.concatenate(
        [meta_tokens.astype(F32), x_sample.reshape(n_dec * dec_seq, d)], axis=0)
    m_small = x_small.shape[0]
    small_segments = [(0, N_META)] + [(N_META + b * dec_seq, dec_seq) for b in range(n_dec)]

    bf = lambda w: w.astype(BF16)

    def col_chunks(w, tile):
        depth_w, rows, cols = w.shape
        return bf(w).reshape(depth_w, rows, cols // tile, tile).transpose(0, 2, 1, 3)

    ffn_w = [(col_chunks(ffn1_wg, FF_TILE), col_chunks(ffn1_wu, FF_TILE), bf(ffn1_wd)),
             (col_chunks(ffn2_wg, FF_TILE), col_chunks(ffn2_wu, FF_TILE), bf(ffn2_wd))]
    a_w_in_b, a_w_out_b, a_gate_w_b = col_chunks(a_w_in, CH_TILE), bf(a_w_out), bf(a_gate_w)
    b_w_in_b, b_w_out_b = col_chunks(b_w_in, CH_TILE), bf(b_w_out)
    a_conv_b3 = a_conv_b[:, None, :]
    a_lambda3 = a_lambda[:, None, :]
    a_gate_b4 = a_gate_b[:, :, None, :]

    p_h, p_ha, p_hb, s_h, s_ha, s_hb = [], [], [], [], [], []
    for layer in range(depth):
        g = norm_g[layer]
        j = layer // 2

        x_main = _ffn(x_main, g[0:2], *ffn_w[0], layer, ROW_TILE)
        x_small = _ffn(x_small, g[0:2], *ffn_w[0], layer, m_small)

        if layer % 2 == 0:
            h0 = jnp.concatenate([jnp.zeros((1, state_a_h.shape[-1]), F32), state_a_h[j]], axis=0)
            hist0 = jnp.concatenate(
                [jnp.zeros((1,) + cache_a_conv.shape[2:], F32), cache_a_conv[j]], axis=0)
            a_args = (a_w_in_b, a_conv_w, a_conv_b3, a_gate_w_b, a_gate_b4, a_lambda3, a_w_out_b)
            x_small, h_s, hist_s = _rglru(x_small, g[2:4], *a_args, h0, hist0,
                                          j, m_small, small_segments)
            h_s, hist_s = h_s[0], hist_s[0]
            x_main, h_m, hist_m = _rglru(x_main, g[2:4], *a_args, h_s[0:1], hist_s[0:1],
                                         j, ROW_TILE, None)
            p_h.append(h_m[-1])
            p_ha.append(hist_m[-1])
            s_h.append(h_s[1:])
            s_ha.append(hist_s[1:])
        else:
            hist0 = jnp.concatenate(
                [jnp.zeros((1,) + cache_b_conv.shape[2:], F32), cache_b_conv[j]], axis=0)
            b_args = (b_w_in_b, b_conv_w, b_w_out_b)
            x_small, hist_s = _shortconv(x_small, g[2:4], *b_args, hist0,
                                         j, m_small, small_segments)
            hist_s = hist_s[0]
            x_main, hist_m = _shortconv(x_main, g[2:4], *b_args, hist_s[0:1],
                                        j, ROW_TILE, None)
            p_hb.append(hist_m[-1])
            s_hb.append(hist_s[1:])

        x_main = _ffn(x_main, g[4:6], *ffn_w[1], layer, ROW_TILE)
        x_small = _ffn(x_small, g[4:6], *ffn_w[1], layer, m_small)

    y_prompt = x_main.reshape(1, seq, d)
    y_sample = x_small[N_META:].reshape(n_dec, dec_seq, d)
    return (y_prompt, y_sample,
            jnp.stack(p_h), jnp.stack(p_ha), jnp.stack(p_hb),
            jnp.stack(s_h), jnp.stack(s_ha), jnp.stack(s_hb))
```

```python
import functools

import jax
import jax.numpy as jnp
from jax import lax
from jax.experimental import pallas as pl
from jax.experimental.pallas import tpu as pltpu

F32 = jnp.float32
BF16 = jnp.bfloat16

EPS = 1e-6
RG_C = 8.0
RG_BLOCK = 128
N_META = 16
CONV_A = 4
CONV_B = 3

LANES = 128
SUBLANES = 8
MXU_COLS = 256
VMEM_LIMIT_BYTES = 60 * 1024 * 1024

ROW_TILE = 512
FFN_ROW_TILE = 1024
FF_TILE = 512
CH_TILE = 512
NORM_ROWS = 16
CONV_ROWS = 64
LOOP_UNROLL = 4


def _dot(a, b):
    return jnp.dot(a, b, preferred_element_type=F32)


def _row_chunks(n_rows, rows_per_chunk, body, unroll=LOOP_UNROLL):
    def step(k, _):
        body(pl.multiple_of(k * rows_per_chunk, rows_per_chunk))
        return 0
    lax.fori_loop(0, n_rows // rows_per_chunk, step, 0, unroll=unroll)


def _lane_tile(v, d):
    return jnp.concatenate([v] * (d // LANES), axis=-1)


def _inv_rms_to(src_ref, scale, ss_ref, inv_ref):
    n_rows, d = src_ref.shape

    def partial(r0):
        x = src_ref[pl.ds(r0, SUBLANES), :]
        x2 = x * x
        acc = x2[:, 0:LANES]
        for q in range(1, d // LANES):
            acc = acc + x2[:, q * LANES:(q + 1) * LANES]
        ss_ref[pl.ds(r0, SUBLANES), :] = acc
    _row_chunks(n_rows, SUBLANES, partial)
    tot = jnp.sum(ss_ref[...], axis=-1, keepdims=True)
    inv = scale * lax.rsqrt(tot * (1.0 / d) + EPS)
    inv_ref[...] = jnp.broadcast_to(inv, inv_ref.shape)


def _prenorm(x_ref, g_ref, xn_ref, ss_ref, inv_ref):
    n_rows, d = x_ref.shape
    _inv_rms_to(x_ref, 1.0, ss_ref, inv_ref)
    gain = jnp.broadcast_to(g_ref[0:1, :], (SUBLANES, d))
    gain = jnp.concatenate([gain] * (NORM_ROWS // SUBLANES), axis=0)

    def apply(r0):
        rows = pl.ds(r0, NORM_ROWS)
        inv = _lane_tile(inv_ref[rows, :], d)
        xn_ref[rows, :] = ((x_ref[rows, :] * inv) * gain).astype(BF16)
    _row_chunks(n_rows, NORM_ROWS, apply)


def _postnorm_residual(x_ref, g_ref, o_ref, scale, ss_ref, inv_ref):
    n_rows, d = x_ref.shape
    _inv_rms_to(o_ref, scale, ss_ref, inv_ref)
    gain = jnp.broadcast_to(g_ref[1:2, :], (SUBLANES, d))

    def apply(r0):
        rows = pl.ds(r0, SUBLANES)
        inv = _lane_tile(inv_ref[rows, :], d)
        o_ref[rows, :] = x_ref[rows, :] + (o_ref[rows, :] * inv) * gain
    _row_chunks(n_rows, SUBLANES, apply)


def _compiler_params():
    return pltpu.CompilerParams(
        dimension_semantics=("arbitrary", "arbitrary"),
        vmem_limit_bytes=VMEM_LIMIT_BYTES)


def _ffn_kernel(x_ref, g_ref, wg_ref, wu_ref, wd_ref, o_ref,
                xn_ref, gate_ref, up_ref, h_ref, ss_ref, inv_ref):
    j = pl.program_id(1)

    @pl.when(j == 0)
    def _():
        _prenorm(x_ref, g_ref, xn_ref, ss_ref, inv_ref)
        o_ref[...] = jnp.zeros_like(o_ref)

    xn = xn_ref[...]
    gate_ref[...] = _dot(xn, wg_ref[...])
    up_ref[...] = _dot(xn, wu_ref[...])
    g = gate_ref[...]
    h_ref[...] = (g * jax.nn.sigmoid(g) * up_ref[...]).astype(BF16)
    o_ref[...] += _dot(h_ref[...], wd_ref[...])

    @pl.when(j == pl.num_programs(1) - 1)
    def _():
        _postnorm_residual(x_ref, g_ref, o_ref, 0.5, ss_ref, inv_ref)


def _ffn(x, gains, wg, wu, wd, layer, row_tile):
    m, d = x.shape
    grid = (m // row_tile, wg.shape[-1] // FF_TILE)
    return pl.pallas_call(
        _ffn_kernel,
        grid=grid,
        in_specs=[
            pl.BlockSpec((row_tile, d), lambda i, j: (i, 0)),
            pl.BlockSpec((2, d), lambda i, j: (0, 0)),
            pl.BlockSpec((None, d, FF_TILE), lambda i, j: (layer, 0, j)),
            pl.BlockSpec((None, d, FF_TILE), lambda i, j: (layer, 0, j)),
            pl.BlockSpec((None, FF_TILE, d), lambda i, j: (layer, j, 0)),
        ],
        out_specs=pl.BlockSpec((row_tile, d), lambda i, j: (i, 0)),
        out_shape=jax.ShapeDtypeStruct((m, d), F32),
        scratch_shapes=[
            pltpu.VMEM((row_tile, d), BF16),
            pltpu.VMEM((row_tile, FF_TILE), F32),
            pltpu.VMEM((row_tile, FF_TILE), F32),
            pltpu.VMEM((row_tile, FF_TILE), BF16),
            pltpu.VMEM((row_tile, LANES), F32),
            pltpu.VMEM((row_tile, LANES), F32),
        ],
        compiler_params=_compiler_params(),
        name="ffn",
    )(x, gains, wg, wu, wd)


def _neg_softplus_neg(lam):
    return -(jnp.maximum(-lam, 0.0) + jnp.log1p(jnp.exp(-jnp.abs(lam))))


def _rglru_coeffs(r_pre, i_pre, u, log_sig_lam):
    r = jax.nn.sigmoid(r_pre)
    ig = jax.nn.sigmoid(i_pre)
    log_a = RG_C * r * log_sig_lam
    a = jnp.exp(log_a)
    one_minus_a2 = -jnp.tanh(log_a) * (a * a + 1.0)
    return a, jnp.sqrt(one_minus_a2) * (ig * u)


def _conv_taps(stage_ref, w_ref, r0, rows, width, cols=slice(None)):
    hw = width - 1
    out = stage_ref[SUBLANES + r0:SUBLANES + r0 + rows, cols] * w_ref[hw:hw + 1, cols]
    for q in range(hw):
        lo = SUBLANES - hw + q + r0
        out = out + stage_ref[lo:lo + rows, cols] * w_ref[q:q + 1, cols]
    return out


def _rglru_main_kernel(x_ref, g_ref, wgate_ref, wu_ref, cw_ref, cb_ref, gw_ref, gb_ref,
                       lam_ref, wo_ref, h0_ref, hist0_ref,
                       o_ref, hlast_ref, histlast_ref,
                       xn_ref, state_ref, gate_ref, stage_ref, u_ref, gm_ref, y_ref,
                       ss_ref, inv_ref):
    i = pl.program_id(0)
    c = pl.program_id(1)
    n_rows = x_ref.shape[0]
    hw = CONV_A - 1
    n_blocks = CH_TILE // RG_BLOCK
    block_cols = [slice(nb * RG_BLOCK, (nb + 1) * RG_BLOCK) for nb in range(n_blocks)]

    @pl.when(c == 0)
    def _():
        _prenorm(x_ref, g_ref, xn_ref, ss_ref, inv_ref)
        o_ref[...] = jnp.zeros_like(o_ref)

    @pl.when(i == 0)
    def _():
        state_ref[c, 0:1, :] = h0_ref[0:1, :]
        state_ref[c, 1:1 + hw, :] = hist0_ref[0]

    xn = xn_ref[...]
    log_sig_lam = _neg_softplus_neg(lam_ref[...])
    sub = lax.broadcasted_iota(jnp.int32, (SUBLANES, RG_BLOCK), 0)
    n_halves = CH_TILE // MXU_COLS
    blocks_per_half = n_blocks // n_halves

    def half_cols(half):
        return slice(half * MXU_COLS, (half + 1) * MXU_COLS)

    def project_gate(half):
        gate_ref[:, half_cols(half)] = _dot(xn, wgate_ref[:, half_cols(half)])

    def project_u(half):
        stage_ref[SUBLANES:SUBLANES + n_rows, half_cols(half)] = _dot(xn, wu_ref[:, half_cols(half)])

    def conv_and_gate_matmuls(half):
        cols = half_cols(half)
        stage_ref[SUBLANES - hw:SUBLANES, cols] = state_ref[c, 1:1 + hw, cols]
        new_hist = stage_ref[SUBLANES + n_rows - hw:SUBLANES + n_rows, cols]
        state_ref[c, 1:1 + hw, cols] = new_hist
        histlast_ref[0, :, cols] = new_hist
        for r0 in range(0, n_rows, CONV_ROWS):
            u_ref[r0:r0 + CONV_ROWS, cols] = (
                _conv_taps(stage_ref, cw_ref, r0, CONV_ROWS, CONV_A, cols) + cb_ref[:, cols])
        for nb in range(half * blocks_per_half, (half + 1) * blocks_per_half):
            ub = u_ref[:, block_cols[nb]].astype(BF16)
            gm_ref[:, 2 * nb * RG_BLOCK:2 * (nb + 1) * RG_BLOCK] = _dot(ub, gw_ref[nb])

    def slab_scan(a, b, h_prev):
        for dist in (1, 2, 4):
            a_up = jnp.where(sub >= dist, pltpu.roll(a, dist, 0), 1.0)
            b_up = jnp.where(sub >= dist, pltpu.roll(b, dist, 0), 0.0)
            b = a * b_up + b
            a = a * a_up
        return b + a * h_prev

    def recurrence(half):
        blocks = range(half * blocks_per_half, (half + 1) * blocks_per_half)
        h_prev = {nb: state_ref[c, 0:1, block_cols[nb]] for nb in blocks}
        for r0 in range(0, n_rows, NORM_ROWS):
            rows = slice(r0, r0 + NORM_ROWS)
            for nb in blocks:
                cols = block_cols[nb]
                g0 = 2 * nb * RG_BLOCK
                bias = gb_ref[nb]
                a, b = _rglru_coeffs(
                    gm_ref[rows, g0:g0 + RG_BLOCK] + bias[:, :RG_BLOCK],
                    gm_ref[rows, g0 + RG_BLOCK:g0 + 2 * RG_BLOCK] + bias[:, RG_BLOCK:],
                    u_ref[rows, cols], log_sig_lam[:, cols])
                h_lo = slab_scan(a[:SUBLANES], b[:SUBLANES], h_prev[nb])
                h_hi = slab_scan(a[SUBLANES:], b[SUBLANES:], h_lo[SUBLANES - 1:SUBLANES, :])
                h = jnp.concatenate([h_lo, h_hi], axis=0)
                y_ref[rows, cols] = (jax.nn.gelu(gate_ref[rows, cols]) * h).astype(BF16)
                h_prev[nb] = h_hi[SUBLANES - 1:SUBLANES, :]
        for nb in blocks:
            state_ref[c, 0:1, block_cols[nb]] = h_prev[nb]
            hlast_ref[0:1, block_cols[nb]] = h_prev[nb]

    def project_out(half):
        o_ref[...] += _dot(y_ref[:, half_cols(half)], wo_ref[half_cols(half), :])

    project_gate(0)
    project_u(0)
    for half in range(n_halves):
        if half + 1 < n_halves:
            project_gate(half + 1)
        conv_and_gate_matmuls(half)
        if half + 1 < n_halves:
            project_u(half + 1)
        if half > 0:
            project_out(half - 1)
        recurrence(half)
    project_out(n_halves - 1)

    @pl.when(c == pl.num_programs(1) - 1)
    def _():
        _postnorm_residual(x_ref, g_ref, o_ref, 1.0, ss_ref, inv_ref)


def _rglru(x, gains, w_in, conv_w, conv_b, gate_w, gate_b, lam, w_out, h0, hist0,
           layer, row_tile, segments):
    m, d = x.shape
    d_rnn = w_out.shape[-2]
    nc = d_rnn // CH_TILE
    nb = CH_TILE // RG_BLOCK
    hw = CONV_A - 1
    n_seq = 1 if segments is None else len(segments)
    grid = (m // row_tile, nc)
    tile_f32 = pltpu.VMEM((row_tile, CH_TILE), F32)
    norm_scratch = [pltpu.VMEM((row_tile, LANES), F32)] * 2
    if segments is None:
        kernel = _rglru_main_kernel
        scratch = [
            pltpu.VMEM((row_tile, d), BF16),
            pltpu.VMEM((nc, SUBLANES, CH_TILE), F32),
            tile_f32,
            pltpu.VMEM((row_tile + SUBLANES, CH_TILE), F32),
            tile_f32,
            pltpu.VMEM((row_tile, 2 * CH_TILE), F32),
            pltpu.VMEM((row_tile, CH_TILE), BF16),
        ] + norm_scratch
    else:
        kernel = functools.partial(_rglru_small_kernel, tuple(segments))
        scratch = [
            pltpu.VMEM((row_tile, d), BF16),
            pltpu.VMEM((row_tile + SUBLANES, CH_TILE), F32),
            tile_f32, tile_f32, tile_f32, tile_f32,
        ] + norm_scratch
    return pl.pallas_call(
        kernel,
        grid=grid,
        in_specs=[
            pl.BlockSpec((row_tile, d), lambda i, c: (i, 0)),
            pl.BlockSpec((2, d), lambda i, c: (0, 0)),
            pl.BlockSpec((None, d, CH_TILE), lambda i, c: (layer, 0, c)),
            pl.BlockSpec((None, d, CH_TILE), lambda i, c: (layer, 0, nc + c)),
            pl.BlockSpec((None, CONV_A, CH_TILE), lambda i, c: (layer, 0, c)),
            pl.BlockSpec((None, 1, CH_TILE), lambda i, c: (layer, 0, c)),
            pl.BlockSpec((None, nb, RG_BLOCK, 2 * RG_BLOCK), lambda i, c: (layer, c, 0, 0)),
            pl.BlockSpec((None, nb, 1, 2 * RG_BLOCK), lambda i, c: (layer, c, 0, 0)),
            pl.BlockSpec((None, 1, CH_TILE), lambda i, c: (layer, 0, c)),
            pl.BlockSpec((None, CH_TILE, d), lambda i, c: (layer, c, 0)),
            pl.BlockSpec((n_seq, CH_TILE), lambda i, c: (0, c)),
            pl.BlockSpec((n_seq, hw, CH_TILE), lambda i, c: (0, 0, c)),
        ],
        out_specs=[
            pl.BlockSpec((row_tile, d), lambda i, c: (i, 0)),
            pl.BlockSpec((None, n_seq, CH_TILE), lambda i, c: (i, 0, c)),
            pl.BlockSpec((None, n_seq, hw, CH_TILE), lambda i, c: (i, 0, 0, c)),
        ],
        out_shape=[
            jax.ShapeDtypeStruct((m, d), F32),
            jax.ShapeDtypeStruct((grid[0], n_seq, d_rnn), F32),
            jax.ShapeDtypeStruct((grid[0], n_seq, hw, d_rnn), F32),
        ],
        scratch_shapes=scratch,
        compiler_params=_compiler_params(),
        name="rglru",
    )(x, gains, w_in, w_in, conv_w, conv_b, gate_w, gate_b, lam, w_out, h0, hist0)


def _segment_conv(stage_ref, seq, hist, w_ref, width):
    n = seq.shape[0]
    hw = width - 1
    stage_ref[SUBLANES - hw:SUBLANES, :] = hist
    stage_ref[SUBLANES:SUBLANES + n, :] = seq
    return (_conv_taps(stage_ref, w_ref, 0, n, width),
            stage_ref[SUBLANES + n - hw:SUBLANES + n, :])


def _rglru_small_kernel(segments,
                        x_ref, g_ref, wgate_ref, wu_ref, cw_ref, cb_ref, gw_ref, gb_ref,
                        lam_ref, wo_ref, h0_ref, hist0_ref,
                        o_ref, hlast_ref, histlast_ref,
                        xn_ref, stage_ref, u_ref, a_ref, b_ref, h_ref, ss_ref, inv_ref):
    c = pl.program_id(1)

    @pl.when(c == 0)
    def _():
        _prenorm(x_ref, g_ref, xn_ref, ss_ref, inv_ref)
        o_ref[...] = jnp.zeros_like(o_ref)

    xn = xn_ref[...]
    gate = _dot(xn, wgate_ref[...])
    u_pre = _dot(xn, wu_ref[...])

    for s, (r0, n) in enumerate(segments):
        conv, new_hist = _segment_conv(stage_ref, u_pre[r0:r0 + n, :], hist0_ref[s], cw_ref, CONV_A)
        u_ref[r0:r0 + n, :] = conv + cb_ref[...]
        histlast_ref[s] = new_hist

    log_sig_lam = _neg_softplus_neg(lam_ref[...])
    for nb in range(CH_TILE // RG_BLOCK):
        cols = slice(nb * RG_BLOCK, (nb + 1) * RG_BLOCK)
        ub = u_ref[:, cols]
        gm = _dot(ub.astype(BF16), gw_ref[nb]) + gb_ref[nb]
        a, b = _rglru_coeffs(gm[:, :RG_BLOCK], gm[:, RG_BLOCK:], ub, log_sig_lam[:, cols])
        a_ref[:, cols] = a
        b_ref[:, cols] = b

    for s, (r0, n) in enumerate(segments):
        def step(t, h):
            h = a_ref[pl.ds(t, 1), :] * h + b_ref[pl.ds(t, 1), :]
            h_ref[pl.ds(t, 1), :] = h
            return h
        hlast_ref[s:s + 1, :] = lax.fori_loop(r0, r0 + n, step, h0_ref[s:s + 1, :])

    y = (jax.nn.gelu(gate) * h_ref[...]).astype(BF16)
    o_ref[...] += _dot(y, wo_ref[...])

    @pl.when(c == pl.num_programs(1) - 1)
    def _():
        _postnorm_residual(x_ref, g_ref, o_ref, 1.0, ss_ref, inv_ref)


def _shortconv_small_kernel(segments,
                            x_ref, g_ref, wb_ref, wc_ref, wv_ref, cw_ref, wo_ref, hist0_ref,
                            o_ref, histlast_ref,
                            xn_ref, stage_ref, y_ref, ss_ref, inv_ref):
    c = pl.program_id(1)

    @pl.when(c == 0)
    def _():
        _prenorm(x_ref, g_ref, xn_ref, ss_ref, inv_ref)
        o_ref[...] = jnp.zeros_like(o_ref)

    xn = xn_ref[...]
    gb = _dot(xn, wb_ref[...])
    cv = _dot(xn, wc_ref[...]) * _dot(xn, wv_ref[...])

    for s, (r0, n) in enumerate(segments):
        z, new_hist = _segment_conv(stage_ref, cv[r0:r0 + n, :], hist0_ref[s], cw_ref, CONV_B)
        y_ref[r0:r0 + n, :] = (gb[r0:r0 + n, :] * z).astype(BF16)
        histlast_ref[s] = new_hist

    o_ref[...] += _dot(y_ref[...], wo_ref[...])

    @pl.when(c == pl.num_programs(1) - 1)
    def _():
        _postnorm_residual(x_ref, g_ref, o_ref, 1.0, ss_ref, inv_ref)


def _shortconv_main_kernel(x_ref, g_ref, wb_ref, wc_ref, wv_ref, cw_ref, wo_ref, hist0_ref,
                           o_ref, histlast_ref,
                           xn_ref, state_ref, b_ref, c_ref, stage_ref, y_ref, ss_ref, inv_ref):
    i = pl.program_id(0)
    c = pl.program_id(1)
    n_rows = x_ref.shape[0]
    hw = CONV_B - 1

    @pl.when(c == 0)
    def _():
        _prenorm(x_ref, g_ref, xn_ref, ss_ref, inv_ref)
        o_ref[...] = jnp.zeros_like(o_ref)

    @pl.when(i == 0)
    def _():
        state_ref[c, 0:hw, :] = hist0_ref[0]

    xn = xn_ref[...]
    b_ref[...] = _dot(xn, wb_ref[...])
    c_ref[...] = _dot(xn, wc_ref[...])
    stage_ref[SUBLANES:SUBLANES + n_rows, :] = _dot(xn, wv_ref[...])

    def cv_chunk(r0):
        rows = pl.ds(pl.multiple_of(SUBLANES + r0, SUBLANES), NORM_ROWS)
        stage_ref[rows, :] = stage_ref[rows, :] * c_ref[pl.ds(r0, NORM_ROWS), :]
    _row_chunks(n_rows, NORM_ROWS, cv_chunk)

    stage_ref[SUBLANES - hw:SUBLANES, :] = state_ref[c, 0:hw, :]
    new_hist = stage_ref[SUBLANES + n_rows - hw:SUBLANES + n_rows, :]
    state_ref[c, 0:hw, :] = new_hist
    histlast_ref[0] = new_hist

    for r0 in range(0, n_rows, CONV_ROWS):
        z = _conv_taps(stage_ref, cw_ref, r0, CONV_ROWS, CONV_B)
        y_ref[r0:r0 + CONV_ROWS, :] = (b_ref[r0:r0 + CONV_ROWS, :] * z).astype(BF16)

    o_ref[...] += _dot(y_ref[...], wo_ref[...])

    @pl.when(c == pl.num_programs(1) - 1)
    def _():
        _postnorm_residual(x_ref, g_ref, o_ref, 1.0, ss_ref, inv_ref)


def _shortconv(x, gains, w_in, conv_w, w_out, hist0, layer, row_tile, segments):
    m, d = x.shape
    d_conv = w_out.shape[-2]
    nc = d_conv // CH_TILE
    hw = CONV_B - 1
    n_seq = 1 if segments is None else len(segments)
    grid = (m // row_tile, nc)
    tile_f32 = pltpu.VMEM((row_tile, CH_TILE), F32)
    norm_scratch = [pltpu.VMEM((row_tile, LANES), F32)] * 2
    if segments is None:
        kernel = _shortconv_main_kernel
        scratch = [
            pltpu.VMEM((row_tile, d), BF16),
            pltpu.VMEM((nc, SUBLANES, CH_TILE), F32),
            tile_f32, tile_f32,
            pltpu.VMEM((row_tile + SUBLANES, CH_TILE), F32),
            pltpu.VMEM((row_tile, CH_TILE), BF16),
        ] + norm_scratch
    else:
        kernel = functools.partial(_shortconv_small_kernel, tuple(segments))
        scratch = [
            pltpu.VMEM((row_tile, d), BF16),
            pltpu.VMEM((row_tile + SUBLANES, CH_TILE), F32),
            pltpu.VMEM((row_tile, CH_TILE), BF16),
        ] + norm_scratch
    return pl.pallas_call(
        kernel,
        grid=grid,
        in_specs=[
            pl.BlockSpec((row_tile, d), lambda i, c: (i, 0)),
            pl.BlockSpec((2, d), lambda i, c: (0, 0)),
            pl.BlockSpec((None, d, CH_TILE), lambda i, c: (layer, 0, c)),
            pl.BlockSpec((None, d, CH_TILE), lambda i, c: (layer, 0, nc + c)),
            pl.BlockSpec((None, d, CH_TILE), lambda i, c: (layer, 0, 2 * nc + c)),
            pl.BlockSpec((None, CONV_B, CH_TILE), lambda i, c: (layer, 0, c)),
            pl.BlockSpec((None, CH_TILE, d), lambda i, c: (layer, c, 0)),
            pl.BlockSpec((n_seq, hw, CH_TILE), lambda i, c: (0, 0, c)),
        ],
        out_specs=[
            pl.BlockSpec((row_tile, d), lambda i, c: (i, 0)),
            pl.BlockSpec((None, n_seq, hw, CH_TILE), lambda i, c: (i, 0, 0, c)),
        ],
        out_shape=[
            jax.ShapeDtypeStruct((m, d), F32),
            jax.ShapeDtypeStruct((grid[0], n_seq, hw, d_conv), F32),
        ],
        scratch_shapes=scratch,
        compiler_params=_compiler_params(),
        name="shortconv",
    )(x, gains, w_in, w_in, w_in, conv_w, w_out, hist0)


def kernel(x_prompt, x_sample, state_a_h, cache_a_conv, cache_b_conv, meta_tokens, norm_g, ffn1_wg, ffn1_wu, ffn1_wd, ffn2_wg, ffn2_wu, ffn2_wd, a_w_in, a_conv_w, a_conv_b, a_gate_w, a_gate_b, a_lambda, a_w_out, b_w_in, b_conv_w, b_w_out):
    depth = norm_g.shape[0]
    n_prompt, seq, d = x_prompt.shape
    n_dec, dec_seq, _ = x_sample.shape
    assert n_prompt == 1, "prompt rows are treated as one causal sequence"
    assert seq % ROW_TILE == 0 and seq % FFN_ROW_TILE == 0

    x_main = x_prompt.reshape(seq, d)
    x_small = jnp.concatenate(
        [meta_tokens.astype(F32), x_sample.reshape(n_dec * dec_seq, d)], axis=0)
    m_small = x_small.shape[0]
    small_segments = [(0, N_META)] + [(N_META + b * dec_seq, dec_seq) for b in range(n_dec)]

    bf = lambda w: w.astype(BF16)

    ffn_w = [(bf(ffn1_wg), bf(ffn1_wu), bf(ffn1_wd)), (bf(ffn2_wg), bf(ffn2_wu), bf(ffn2_wd))]
    a_w_in_b, a_w_out_b, a_gate_w_b = bf(a_w_in), bf(a_w_out), bf(a_gate_w)
    b_w_in_b, b_w_out_b = bf(b_w_in), bf(b_w_out)
    a_conv_b3 = a_conv_b[:, None, :]
    a_lambda3 = a_lambda[:, None, :]
    a_gate_b4 = a_gate_b[:, :, None, :]

    p_h, p_ha, p_hb, s_h, s_ha, s_hb = [], [], [], [], [], []
    for layer in range(depth):
        g = norm_g[layer]
        j = layer // 2

        x_main = _ffn(x_main, g[0:2], *ffn_w[0], layer, FFN_ROW_TILE)
        x_small = _ffn(x_small, g[0:2], *ffn_w[0], layer, m_small)

        if layer % 2 == 0:
            h0 = jnp.concatenate([jnp.zeros((1, state_a_h.shape[-1]), F32), state_a_h[j]], axis=0)
            hist0 = jnp.concatenate(
                [jnp.zeros((1,) + cache_a_conv.shape[2:], F32), cache_a_conv[j]], axis=0)
            a_args = (a_w_in_b, a_conv_w, a_conv_b3, a_gate_w_b, a_gate_b4, a_lambda3, a_w_out_b)
            x_small, h_s, hist_s = _rglru(x_small, g[2:4], *a_args, h0, hist0,
                                          j, m_small, small_segments)
            h_s, hist_s = h_s[0], hist_s[0]
            x_main, h_m, hist_m = _rglru(x_main, g[2:4], *a_args, h_s[0:1], hist_s[0:1],
                                         j, ROW_TILE, None)
            p_h.append(h_m[-1])
            p_ha.append(hist_m[-1])
            s_h.append(h_s[1:])
            s_ha.append(hist_s[1:])
        else:
            hist0 = jnp.concatenate(
                [jnp.zeros((1,) + cache_b_conv.shape[2:], F32), cache_b_conv[j]], axis=0)
            b_args = (b_w_in_b, b_conv_w, b_w_out_b)
            x_small, hist_s = _shortconv(x_small, g[2:4], *b_args, hist0,
                                         j, m_small, small_segments)
            hist_s = hist_s[0]
            x_main, hist_m = _shortconv(x_main, g[2:4], *b_args, hist_s[0:1],
                                        j, ROW_TILE, None)
            p_hb.append(hist_m[-1])
            s_hb.append(hist_s[1:])

        x_main = _ffn(x_main, g[4:6], *ffn_w[1], layer, FFN_ROW_TILE)
        x_small = _ffn(x_small, g[4:6], *ffn_w[1], layer, m_small)

    y_prompt = x_main.reshape(1, seq, d)
    y_sample = x_small[N_META:].reshape(n_dec, dec_seq, d)
    return (y_prompt, y_sample,
            jnp.stack(p_h), jnp.stack(p_ha), jnp.stack(p_hb),
            jnp.stack(s_h), jnp.stack(s_ha), jnp.stack(s_hb))
```

```python
import functools

import jax
import jax.numpy as jnp
from jax import lax
from jax.experimental import pallas as pl
from jax.experimental.pallas import tpu as pltpu

F32 = jnp.float32
BF16 = jnp.bfloat16

EPS = 1e-6
RG_C = 8.0
RG_BLOCK = 128
N_META = 16
CONV_A = 4
CONV_B = 3

LANES = 128
SUBLANES = 8
MXU_COLS = 256
VMEM_LIMIT_BYTES = 60 * 1024 * 1024

ROW_TILE = 512
FFN_ROW_TILE = 1024
FF_TILE = 512
CH_TILE = 512
NORM_ROWS = 16
CONV_ROWS = 64
LOOP_UNROLL = 4


def _dot(a, b):
    return jnp.dot(a, b, preferred_element_type=F32)


def _row_chunks(n_rows, rows_per_chunk, body, unroll=LOOP_UNROLL):
    def step(k, _):
        body(pl.multiple_of(k * rows_per_chunk, rows_per_chunk))
        return 0
    lax.fori_loop(0, n_rows // rows_per_chunk, step, 0, unroll=unroll)


def _lane_tile(v, d):
    return jnp.concatenate([v] * (d // LANES), axis=-1)


def _inv_rms_to(src_ref, scale, ss_ref, inv_ref):
    n_rows, d = src_ref.shape

    def partial(r0):
        x = src_ref[pl.ds(r0, SUBLANES), :]
        x2 = x * x
        acc = x2[:, 0:LANES]
        for q in range(1, d // LANES):
            acc = acc + x2[:, q * LANES:(q + 1) * LANES]
        ss_ref[pl.ds(r0, SUBLANES), :] = acc
    _row_chunks(n_rows, SUBLANES, partial)
    tot = jnp.sum(ss_ref[...], axis=-1, keepdims=True)
    inv = scale * lax.rsqrt(tot * (1.0 / d) + EPS)
    inv_ref[...] = jnp.broadcast_to(inv, inv_ref.shape)


def _prenorm(x_ref, g_ref, xn_ref, ss_ref, inv_ref):
    n_rows, d = x_ref.shape
    _inv_rms_to(x_ref, 1.0, ss_ref, inv_ref)
    gain = jnp.broadcast_to(g_ref[0:1, :], (SUBLANES, d))
    gain = jnp.concatenate([gain] * (NORM_ROWS // SUBLANES), axis=0)

    def apply(r0):
        rows = pl.ds(r0, NORM_ROWS)
        inv = _lane_tile(inv_ref[rows, :], d)
        xn_ref[rows, :] = ((x_ref[rows, :] * inv) * gain).astype(BF16)
    _row_chunks(n_rows, NORM_ROWS, apply)


def _postnorm_residual(x_ref, g_ref, o_ref, scale, ss_ref, inv_ref):
    n_rows, d = x_ref.shape
    _inv_rms_to(o_ref, scale, ss_ref, inv_ref)
    gain = jnp.broadcast_to(g_ref[1:2, :], (SUBLANES, d))

    def apply(r0):
        rows = pl.ds(r0, SUBLANES)
        inv = _lane_tile(inv_ref[rows, :], d)
        o_ref[rows, :] = x_ref[rows, :] + (o_ref[rows, :] * inv) * gain
    _row_chunks(n_rows, SUBLANES, apply)


def _compiler_params():
    return pltpu.CompilerParams(
        dimension_semantics=("arbitrary", "arbitrary"),
        vmem_limit_bytes=VMEM_LIMIT_BYTES)


def _ffn_kernel(x_ref, g_ref, wg_ref, wu_ref, wd_ref, o_ref,
                xn_ref, gate_ref, up_ref, h_ref, ss_ref, inv_ref):
    j = pl.program_id(1)

    @pl.when(j == 0)
    def _():
        _prenorm(x_ref, g_ref, xn_ref, ss_ref, inv_ref)
        o_ref[...] = jnp.zeros_like(o_ref)

    xn = xn_ref[...]
    n_halves = FF_TILE // MXU_COLS

    def half_cols(half):
        return slice(half * MXU_COLS, (half + 1) * MXU_COLS)

    def project(half):
        cols = half_cols(half)
        gate_ref[:, cols] = _dot(xn, wg_ref[:, cols])
        up_ref[:, cols] = _dot(xn, wu_ref[:, cols])

    def swiglu(half):
        cols = half_cols(half)
        g = gate_ref[:, cols]
        h_ref[:, cols] = (g * jax.nn.sigmoid(g) * up_ref[:, cols]).astype(BF16)

    def project_down(half):
        o_ref[...] += _dot(h_ref[:, half_cols(half)], wd_ref[half_cols(half), :])

    project(0)
    for half in range(n_halves):
        if half + 1 < n_halves:
            project(half + 1)
        swiglu(half)
        project_down(half)

    @pl.when(j == pl.num_programs(1) - 1)
    def _():
        _postnorm_residual(x_ref, g_ref, o_ref, 0.5, ss_ref, inv_ref)


def _ffn(x, gains, wg, wu, wd, layer, row_tile):
    m, d = x.shape
    grid = (m // row_tile, wg.shape[-1] // FF_TILE)
    return pl.pallas_call(
        _ffn_kernel,
        grid=grid,
        in_specs=[
            pl.BlockSpec((row_tile, d), lambda i, j: (i, 0)),
            pl.BlockSpec((2, d), lambda i, j: (0, 0)),
            pl.BlockSpec((None, d, FF_TILE), lambda i, j: (layer, 0, j)),
            pl.BlockSpec((None, d, FF_TILE), lambda i, j: (layer, 0, j)),
            pl.BlockSpec((None, FF_TILE, d), lambda i, j: (layer, j, 0)),
        ],
        out_specs=pl.BlockSpec((row_tile, d), lambda i, j: (i, 0)),
        out_shape=jax.ShapeDtypeStruct((m, d), F32),
        scratch_shapes=[
            pltpu.VMEM((row_tile, d), BF16),
            pltpu.VMEM((row_tile, FF_TILE), F32),
            pltpu.VMEM((row_tile, FF_TILE), F32),
            pltpu.VMEM((row_tile, FF_TILE), BF16),
            pltpu.VMEM((row_tile, LANES), F32),
            pltpu.VMEM((row_tile, LANES), F32),
        ],
        compiler_params=_compiler_params(),
        name="ffn",
    )(x, gains, wg, wu, wd)


def _neg_softplus_neg(lam):
    return -(jnp.maximum(-lam, 0.0) + jnp.log1p(jnp.exp(-jnp.abs(lam))))


def _rglru_coeffs(r_pre, i_pre, u, log_sig_lam):
    r = jax.nn.sigmoid(r_pre)
    ig = jax.nn.sigmoid(i_pre)
    log_a = RG_C * r * log_sig_lam
    a = jnp.exp(log_a)
    one_minus_a2 = -jnp.tanh(log_a) * (a * a + 1.0)
    return a, jnp.sqrt(one_minus_a2) * (ig * u)


def _conv_taps(stage_ref, w_ref, r0, rows, width, cols=slice(None)):
    hw = width - 1
    out = stage_ref[SUBLANES + r0:SUBLANES + r0 + rows, cols] * w_ref[hw:hw + 1, cols]
    for q in range(hw):
        lo = SUBLANES - hw + q + r0
        out = out + stage_ref[lo:lo + rows, cols] * w_ref[q:q + 1, cols]
    return out


def _rglru_main_kernel(x_ref, g_ref, wgate_ref, wu_ref, cw_ref, cb_ref, gw_ref, gb_ref,
                       lam_ref, wo_ref, h0_ref, hist0_ref,
                       o_ref, hlast_ref, histlast_ref,
                       xn_ref, state_ref, gate_ref, stage_ref, u_ref, gm_ref, y_ref,
                       ss_ref, inv_ref):
    i = pl.program_id(0)
    c = pl.program_id(1)
    n_rows = x_ref.shape[0]
    hw = CONV_A - 1
    n_blocks = CH_TILE // RG_BLOCK
    block_cols = [slice(nb * RG_BLOCK, (nb + 1) * RG_BLOCK) for nb in range(n_blocks)]

    @pl.when(c == 0)
    def _():
        _prenorm(x_ref, g_ref, xn_ref, ss_ref, inv_ref)
        o_ref[...] = jnp.zeros_like(o_ref)

    @pl.when(i == 0)
    def _():
        state_ref[c, 0:1, :] = h0_ref[0:1, :]
        state_ref[c, 1:1 + hw, :] = hist0_ref[0]

    xn = xn_ref[...]
    log_sig_lam = _neg_softplus_neg(lam_ref[...])
    sub = lax.broadcasted_iota(jnp.int32, (SUBLANES, RG_BLOCK), 0)
    n_halves = CH_TILE // MXU_COLS
    blocks_per_half = n_blocks // n_halves

    def half_cols(half):
        return slice(half * MXU_COLS, (half + 1) * MXU_COLS)

    def project_gate(half):
        gate_ref[:, half_cols(half)] = _dot(xn, wgate_ref[:, half_cols(half)])

    def project_u(half):
        stage_ref[SUBLANES:SUBLANES + n_rows, half_cols(half)] = _dot(xn, wu_ref[:, half_cols(half)])

    def conv_and_gate_matmuls(half):
        cols = half_cols(half)
        stage_ref[SUBLANES - hw:SUBLANES, cols] = state_ref[c, 1:1 + hw, cols]
        new_hist = stage_ref[SUBLANES + n_rows - hw:SUBLANES + n_rows, cols]
        state_ref[c, 1:1 + hw, cols] = new_hist
        histlast_ref[0, :, cols] = new_hist
        for r0 in range(0, n_rows, CONV_ROWS):
            u_ref[r0:r0 + CONV_ROWS, cols] = (
                _conv_taps(stage_ref, cw_ref, r0, CONV_ROWS, CONV_A, cols) + cb_ref[:, cols])
        for nb in range(half * blocks_per_half, (half + 1) * blocks_per_half):
            ub = u_ref[:, block_cols[nb]].astype(BF16)
            gm_ref[:, 2 * nb * RG_BLOCK:2 * (nb + 1) * RG_BLOCK] = _dot(ub, gw_ref[nb])

    def slab_scan(a, b, h_prev):
        for dist in (1, 2, 4):
            a_up = jnp.where(sub >= dist, pltpu.roll(a, dist, 0), 1.0)
            b_up = jnp.where(sub >= dist, pltpu.roll(b, dist, 0), 0.0)
            b = a * b_up + b
            a = a * a_up
        return b + a * h_prev

    def recurrence(half):
        blocks = range(half * blocks_per_half, (half + 1) * blocks_per_half)
        h_prev = {nb: state_ref[c, 0:1, block_cols[nb]] for nb in blocks}
        for r0 in range(0, n_rows, NORM_ROWS):
            rows = slice(r0, r0 + NORM_ROWS)
            for nb in blocks:
                cols = block_cols[nb]
                g0 = 2 * nb * RG_BLOCK
                bias = gb_ref[nb]
                a, b = _rglru_coeffs(
                    gm_ref[rows, g0:g0 + RG_BLOCK] + bias[:, :RG_BLOCK],
                    gm_ref[rows, g0 + RG_BLOCK:g0 + 2 * RG_BLOCK] + bias[:, RG_BLOCK:],
                    u_ref[rows, cols], log_sig_lam[:, cols])
                h_lo = slab_scan(a[:SUBLANES], b[:SUBLANES], h_prev[nb])
                h_hi = slab_scan(a[SUBLANES:], b[SUBLANES:], h_lo[SUBLANES - 1:SUBLANES, :])
                h = jnp.concatenate([h_lo, h_hi], axis=0)
                y_ref[rows, cols] = (jax.nn.gelu(gate_ref[rows, cols]) * h).astype(BF16)
                h_prev[nb] = h_hi[SUBLANES - 1:SUBLANES, :]
        for nb in blocks:
            state_ref[c, 0:1, block_cols[nb]] = h_prev[nb]
            hlast_ref[0:1, block_cols[nb]] = h_prev[nb]

    def project_out(half):
        o_ref[...] += _dot(y_ref[:, half_cols(half)], wo_ref[half_cols(half), :])

    project_gate(0)
    project_u(0)
    for half in range(n_halves):
        if half + 1 < n_halves:
            project_gate(half + 1)
        conv_and_gate_matmuls(half)
        if half + 1 < n_halves:
            project_u(half + 1)
        if half > 0:
            project_out(half - 1)
        recurrence(half)
    project_out(n_halves - 1)

    @pl.when(c == pl.num_programs(1) - 1)
    def _():
        _postnorm_residual(x_ref, g_ref, o_ref, 1.0, ss_ref, inv_ref)


def _rglru(x, gains, w_in, conv_w, conv_b, gate_w, gate_b, lam, w_out, h0, hist0,
           layer, row_tile, segments):
    m, d = x.shape
    d_rnn = w_out.shape[-2]
    nc = d_rnn // CH_TILE
    nb = CH_TILE // RG_BLOCK
    hw = CONV_A - 1
    n_seq = 1 if segments is None else len(segments)
    grid = (m // row_tile, nc)
    tile_f32 = pltpu.VMEM((row_tile, CH_TILE), F32)
    norm_scratch = [pltpu.VMEM((row_tile, LANES), F32)] * 2
    if segments is None:
        kernel = _rglru_main_kernel
        scratch = [
            pltpu.VMEM((row_tile, d), BF16),
            pltpu.VMEM((nc, SUBLANES, CH_TILE), F32),
            tile_f32,
            pltpu.VMEM((row_tile + SUBLANES, CH_TILE), F32),
            tile_f32,
            pltpu.VMEM((row_tile, 2 * CH_TILE), F32),
            pltpu.VMEM((row_tile, CH_TILE), BF16),
        ] + norm_scratch
    else:
        kernel = functools.partial(_rglru_small_kernel, tuple(segments))
        scratch = [
            pltpu.VMEM((row_tile, d), BF16),
            pltpu.VMEM((row_tile + SUBLANES, CH_TILE), F32),
            tile_f32, tile_f32, tile_f32, tile_f32,
        ] + norm_scratch
    return pl.pallas_call(
        kernel,
        grid=grid,
        in_specs=[
            pl.BlockSpec((row_tile, d), lambda i, c: (i, 0)),
            pl.BlockSpec((2, d), lambda i, c: (0, 0)),
            pl.BlockSpec((None, d, CH_TILE), lambda i, c: (layer, 0, c)),
            pl.BlockSpec((None, d, CH_TILE), lambda i, c: (layer, 0, nc + c)),
            pl.BlockSpec((None, CONV_A, CH_TILE), lambda i, c: (layer, 0, c)),
            pl.BlockSpec((None, 1, CH_TILE), lambda i, c: (layer, 0, c)),
            pl.BlockSpec((None, nb, RG_BLOCK, 2 * RG_BLOCK), lambda i, c: (layer, c, 0, 0)),
            pl.BlockSpec((None, nb, 1, 2 * RG_BLOCK), lambda i, c: (layer, c, 0, 0)),
            pl.BlockSpec((None, 1, CH_TILE), lambda i, c: (layer, 0, c)),
            pl.BlockSpec((None, CH_TILE, d), lambda i, c: (layer, c, 0)),
            pl.BlockSpec((n_seq, CH_TILE), lambda i, c: (0, c)),
            pl.BlockSpec((n_seq, hw, CH_TILE), lambda i, c: (0, 0, c)),
        ],
        out_specs=[
            pl.BlockSpec((row_tile, d), lambda i, c: (i, 0)),
            pl.BlockSpec((None, n_seq, CH_TILE), lambda i, c: (i, 0, c)),
            pl.BlockSpec((None, n_seq, hw, CH_TILE), lambda i, c: (i, 0, 0, c)),
        ],
        out_shape=[
            jax.ShapeDtypeStruct((m, d), F32),
            jax.ShapeDtypeStruct((grid[0], n_seq, d_rnn), F32),
            jax.ShapeDtypeStruct((grid[0], n_seq, hw, d_rnn), F32),
        ],
        scratch_shapes=scratch,
        compiler_params=_compiler_params(),
        name="rglru",
    )(x, gains, w_in, w_in, conv_w, conv_b, gate_w, gate_b, lam, w_out, h0, hist0)


def _segment_conv(stage_ref, seq, hist, w_ref, width):
    n = seq.shape[0]
    hw = width - 1
    stage_ref[SUBLANES - hw:SUBLANES, :] = hist
    stage_ref[SUBLANES:SUBLANES + n, :] = seq
    return (_conv_taps(stage_ref, w_ref, 0, n, width),
            stage_ref[SUBLANES + n - hw:SUBLANES + n, :])


def _rglru_small_kernel(segments,
                        x_ref, g_ref, wgate_ref, wu_ref, cw_ref, cb_ref, gw_ref, gb_ref,
                        lam_ref, wo_ref, h0_ref, hist0_ref,
                        o_ref, hlast_ref, histlast_ref,
                        xn_ref, stage_ref, u_ref, a_ref, b_ref, h_ref, ss_ref, inv_ref):
    c = pl.program_id(1)

    @pl.when(c == 0)
    def _():
        _prenorm(x_ref, g_ref, xn_ref, ss_ref, inv_ref)
        o_ref[...] = jnp.zeros_like(o_ref)

    xn = xn_ref[...]
    gate = _dot(xn, wgate_ref[...])
    u_pre = _dot(xn, wu_ref[...])

    for s, (r0, n) in enumerate(segments):
        conv, new_hist = _segment_conv(stage_ref, u_pre[r0:r0 + n, :], hist0_ref[s], cw_ref, CONV_A)
        u_ref[r0:r0 + n, :] = conv + cb_ref[...]
        histlast_ref[s] = new_hist

    log_sig_lam = _neg_softplus_neg(lam_ref[...])
    for nb in range(CH_TILE // RG_BLOCK):
        cols = slice(nb * RG_BLOCK, (nb + 1) * RG_BLOCK)
        ub = u_ref[:, cols]
        gm = _dot(ub.astype(BF16), gw_ref[nb]) + gb_ref[nb]
        a, b = _rglru_coeffs(gm[:, :RG_BLOCK], gm[:, RG_BLOCK:], ub, log_sig_lam[:, cols])
        a_ref[:, cols] = a
        b_ref[:, cols] = b

    for s, (r0, n) in enumerate(segments):
        def step(t, h):
            h = a_ref[pl.ds(t, 1), :] * h + b_ref[pl.ds(t, 1), :]
            h_ref[pl.ds(t, 1), :] = h
            return h
        hlast_ref[s:s + 1, :] = lax.fori_loop(r0, r0 + n, step, h0_ref[s:s + 1, :])

    y = (jax.nn.gelu(gate) * h_ref[...]).astype(BF16)
    o_ref[...] += _dot(y, wo_ref[...])

    @pl.when(c == pl.num_programs(1) - 1)
    def _():
        _postnorm_residual(x_ref, g_ref, o_ref, 1.0, ss_ref, inv_ref)


def _shortconv_small_kernel(segments,
                            x_ref, g_ref, wb_ref, wc_ref, wv_ref, cw_ref, wo_ref, hist0_ref,
                            o_ref, histlast_ref,
                            xn_ref, stage_ref, y_ref, ss_ref, inv_ref):
    c = pl.program_id(1)

    @pl.when(c == 0)
    def _():
        _prenorm(x_ref, g_ref, xn_ref, ss_ref, inv_ref)
        o_ref[...] = jnp.zeros_like(o_ref)

    xn = xn_ref[...]
    gb = _dot(xn, wb_ref[...])
    cv = _dot(xn, wc_ref[...]) * _dot(xn, wv_ref[...])

    for s, (r0, n) in enumerate(segments):
        z, new_hist = _segment_conv(stage_ref, cv[r0:r0 + n, :], hist0_ref[s], cw_ref, CONV_B)
        y_ref[r0:r0 + n, :] = (gb[r0:r0 + n, :] * z).astype(BF16)
        histlast_ref[s] = new_hist

    o_ref[...] += _dot(y_ref[...], wo_ref[...])

    @pl.when(c == pl.num_programs(1) - 1)
    def _():
        _postnorm_residual(x_ref, g_ref, o_ref, 1.0, ss_ref, inv_ref)


def _shortconv_main_kernel(x_ref, g_ref, wb_ref, wc_ref, wv_ref, cw_ref, wo_ref, hist0_ref,
                           o_ref, histlast_ref,
                           xn_ref, state_ref, b_ref, c_ref, stage_ref, y_ref, ss_ref, inv_ref):
    i = pl.program_id(0)
    c = pl.program_id(1)
    n_rows = x_ref.shape[0]
    hw = CONV_B - 1

    @pl.when(c == 0)
    def _():
        _prenorm(x_ref, g_ref, xn_ref, ss_ref, inv_ref)
        o_ref[...] = jnp.zeros_like(o_ref)

    @pl.when(i == 0)
    def _():
        state_ref[c, 0:hw, :] = hist0_ref[0]

    xn = xn_ref[...]
    n_halves = CH_TILE // MXU_COLS

    def half_cols(half):
        return slice(half * MXU_COLS, (half + 1) * MXU_COLS)

    def project(half):
        cols = half_cols(half)
        b_ref[:, cols] = _dot(xn, wb_ref[:, cols])
        c_ref[:, cols] = _dot(xn, wc_ref[:, cols])
        stage_ref[SUBLANES:SUBLANES + n_rows, cols] = _dot(xn, wv_ref[:, cols])

    def conv_and_gate(half):
        cols = half_cols(half)
        for r0 in range(0, n_rows, CONV_ROWS):
            rows = slice(SUBLANES + r0, SUBLANES + r0 + CONV_ROWS)
            stage_ref[rows, cols] = stage_ref[rows, cols] * c_ref[r0:r0 + CONV_ROWS, cols]
        stage_ref[SUBLANES - hw:SUBLANES, cols] = state_ref[c, 0:hw, cols]
        new_hist = stage_ref[SUBLANES + n_rows - hw:SUBLANES + n_rows, cols]
        state_ref[c, 0:hw, cols] = new_hist
        histlast_ref[0, :, cols] = new_hist
        for r0 in range(0, n_rows, CONV_ROWS):
            z = _conv_taps(stage_ref, cw_ref, r0, CONV_ROWS, CONV_B, cols)
            y_ref[r0:r0 + CONV_ROWS, cols] = (b_ref[r0:r0 + CONV_ROWS, cols] * z).astype(BF16)

    def project_out(half):
        o_ref[...] += _dot(y_ref[:, half_cols(half)], wo_ref[half_cols(half), :])

    project(0)
    for half in range(n_halves):
        if half + 1 < n_halves:
            project(half + 1)
        conv_and_gate(half)
        project_out(half)

    @pl.when(c == pl.num_programs(1) - 1)
    def _():
        _postnorm_residual(x_ref, g_ref, o_ref, 1.0, ss_ref, inv_ref)


def _shortconv(x, gains, w_in, conv_w, w_out, hist0, layer, row_tile, segments):
    m, d = x.shape
    d_conv = w_out.shape[-2]
    nc = d_conv // CH_TILE
    hw = CONV_B - 1
    n_seq = 1 if segments is None else len(segments)
    grid = (m // row_tile, nc)
    tile_f32 = pltpu.VMEM((row_tile, CH_TILE), F32)
    norm_scratch = [pltpu.VMEM((row_tile, LANES), F32)] * 2
    if segments is None:
        kernel = _shortconv_main_kernel
        scratch = [
            pltpu.VMEM((row_tile, d), BF16),
            pltpu.VMEM((nc, SUBLANES, CH_TILE), F32),
            tile_f32, tile_f32,
            pltpu.VMEM((row_tile + SUBLANES, CH_TILE), F32),
            pltpu.VMEM((row_tile, CH_TILE), BF16),
        ] + norm_scratch
    else:
        kernel = functools.partial(_shortconv_small_kernel, tuple(segments))
        scratch = [
            pltpu.VMEM((row_tile, d), BF16),
            pltpu.VMEM((row_tile + SUBLANES, CH_TILE), F32),
            pltpu.VMEM((row_tile, CH_TILE), BF16),
        ] + norm_scratch
    return pl.pallas_call(
        kernel,
        grid=grid,
        in_specs=[
            pl.BlockSpec((row_tile, d), lambda i, c: (i, 0)),
            pl.BlockSpec((2, d), lambda i, c: (0, 0)),
            pl.BlockSpec((None, d, CH_TILE), lambda i, c: (layer, 0, c)),
            pl.BlockSpec((None, d, CH_TILE), lambda i, c: (layer, 0, nc + c)),
            pl.BlockSpec((None, d, CH_TILE), lambda i, c: (layer, 0, 2 * nc + c)),
            pl.BlockSpec((None, CONV_B, CH_TILE), lambda i, c: (layer, 0, c)),
            pl.BlockSpec((None, CH_TILE, d), lambda i, c: (layer, c, 0)),
            pl.BlockSpec((n_seq, hw, CH_TILE), lambda i, c: (0, 0, c)),
        ],
        out_specs=[
            pl.BlockSpec((row_tile, d), lambda i, c: (i, 0)),
            pl.BlockSpec((None, n_seq, hw, CH_TILE), lambda i, c: (i, 0, 0, c)),
        ],
        out_shape=[
            jax.ShapeDtypeStruct((m, d), F32),
            jax.ShapeDtypeStruct((grid[0], n_seq, hw, d_conv), F32),
        ],
        scratch_shapes=scratch,
        compiler_params=_compiler_params(),
        name="shortconv",
    )(x, gains, w_in, w_in, w_in, conv_w, w_out, hist0)


def kernel(x_prompt, x_sample, state_a_h, cache_a_conv, cache_b_conv, meta_tokens, norm_g, ffn1_wg, ffn1_wu, ffn1_wd, ffn2_wg, ffn2_wu, ffn2_wd, a_w_in, a_conv_w, a_conv_b, a_gate_w, a_gate_b, a_lambda, a_w_out, b_w_in, b_conv_w, b_w_out):
    depth = norm_g.shape[0]
    n_prompt, seq, d = x_prompt.shape
    n_dec, dec_seq, _ = x_sample.shape
    assert n_prompt == 1, "prompt rows are treated as one causal sequence"
    assert seq % ROW_TILE == 0 and seq % FFN_ROW_TILE == 0

    x_main = x_prompt.reshape(seq, d)
    x_small = jnp.concatenate(
        [meta_tokens.astype(F32), x_sample.reshape(n_dec * dec_seq, d)], axis=0)
    m_small = x_small.shape[0]
    small_segments = [(0, N_META)] + [(N_META + b * dec_seq, dec_seq) for b in range(n_dec)]

    bf = lambda w: w.astype(BF16)

    ffn_w = [(bf(ffn1_wg), bf(ffn1_wu), bf(ffn1_wd)), (bf(ffn2_wg), bf(ffn2_wu), bf(ffn2_wd))]
    a_w_in_b, a_w_out_b, a_gate_w_b = bf(a_w_in), bf(a_w_out), bf(a_gate_w)
    b_w_in_b, b_w_out_b = bf(b_w_in), bf(b_w_out)
    a_conv_b3 = a_conv_b[:, None, :]
    a_lambda3 = a_lambda[:, None, :]
    a_gate_b4 = a_gate_b[:, :, None, :]

    p_h, p_ha, p_hb, s_h, s_ha, s_hb = [], [], [], [], [], []
    for layer in range(depth):
        g = norm_g[layer]
        j = layer // 2

        x_main = _ffn(x_main, g[0:2], *ffn_w[0], layer, FFN_ROW_TILE)
        x_small = _ffn(x_small, g[0:2], *ffn_w[0], layer, m_small)

        if layer % 2 == 0:
            h0 = jnp.concatenate([jnp.zeros((1, state_a_h.shape[-1]), F32), state_a_h[j]], axis=0)
            hist0 = jnp.concatenate(
                [jnp.zeros((1,) + cache_a_conv.shape[2:], F32), cache_a_conv[j]], axis=0)
            a_args = (a_w_in_b, a_conv_w, a_conv_b3, a_gate_w_b, a_gate_b4, a_lambda3, a_w_out_b)
            x_small, h_s, hist_s = _rglru(x_small, g[2:4], *a_args, h0, hist0,
                                          j, m_small, small_segments)
            h_s, hist_s = h_s[0], hist_s[0]
            x_main, h_m, hist_m = _rglru(x_main, g[2:4], *a_args, h_s[0:1], hist_s[0:1],
                                         j, ROW_TILE, None)
            p_h.append(h_m[-1])
            p_ha.append(hist_m[-1])
            s_h.append(h_s[1:])
            s_ha.append(hist_s[1:])
        else:
            hist0 = jnp.concatenate(
                [jnp.zeros((1,) + cache_b_conv.shape[2:], F32), cache_b_conv[j]], axis=0)
            b_args = (b_w_in_b, b_conv_w, b_w_out_b)
            x_small, hist_s = _shortconv(x_small, g[2:4], *b_args, hist0,
                                         j, m_small, small_segments)
            hist_s = hist_s[0]
            x_main, hist_m = _shortconv(x_main, g[2:4], *b_args, hist_s[0:1],
                                        j, ROW_TILE, None)
            p_hb.append(hist_m[-1])
            s_hb.append(hist_s[1:])

        x_main = _ffn(x_main, g[4:6], *ffn_w[1], layer, FFN_ROW_TILE)
        x_small = _ffn(x_small, g[4:6], *ffn_w[1], layer, m_small)

    y_prompt = x_main.reshape(1, seq, d)
    y_sample = x_small[N_META:].reshape(n_dec, dec_seq, d)
    return (y_prompt, y_sample,
            jnp.stack(p_h), jnp.stack(p_ha), jnp.stack(p_hb),
            jnp.stack(s_h), jnp.stack(s_ha), jnp.stack(s_hb))
```

```python
import functools

import jax
import jax.numpy as jnp
from jax import lax
from jax.experimental import pallas as pl
from jax.experimental.pallas import tpu as pltpu

F32 = jnp.float32
BF16 = jnp.bfloat16

EPS = 1e-6
RG_C = 8.0
RG_BLOCK = 128
N_META = 16
CONV_A = 4
CONV_B = 3

LANES = 128
SUBLANES = 8
MXU_COLS = 256
VMEM_LIMIT_BYTES = 60 * 1024 * 1024

ROW_TILE = 512
FFN_ROW_TILE = 1024
FF_TILE = 512
CH_TILE = 512
NORM_ROWS = 16
CONV_ROWS = 64
LOOP_UNROLL = 8


def _dot(a, b):
    return jnp.dot(a, b, preferred_element_type=F32)


def _row_chunks(n_rows, rows_per_chunk, body, unroll=LOOP_UNROLL):
    def step(k, _):
        body(pl.multiple_of(k * rows_per_chunk, rows_per_chunk))
        return 0
    lax.fori_loop(0, n_rows // rows_per_chunk, step, 0, unroll=unroll)


def _lane_tile(v, d):
    return jnp.concatenate([v] * (d // LANES), axis=-1)


def _inv_rms_to(src_ref, scale, ss_ref, inv_ref):
    n_rows, d = src_ref.shape

    def partial(r0):
        x = src_ref[pl.ds(r0, SUBLANES), :]
        x2 = x * x
        acc = x2[:, 0:LANES]
        for q in range(1, d // LANES):
            acc = acc + x2[:, q * LANES:(q + 1) * LANES]
        ss_ref[pl.ds(r0, SUBLANES), :] = acc
    _row_chunks(n_rows, SUBLANES, partial)
    tot = jnp.sum(ss_ref[...], axis=-1, keepdims=True)
    inv = scale * lax.rsqrt(tot * (1.0 / d) + EPS)
    inv_ref[...] = jnp.broadcast_to(inv, inv_ref.shape)


def _prenorm(x_ref, g_ref, xn_ref, ss_ref, inv_ref):
    n_rows, d = x_ref.shape
    _inv_rms_to(x_ref, 1.0, ss_ref, inv_ref)
    gain = jnp.broadcast_to(g_ref[0:1, :], (SUBLANES, d))
    gain = jnp.concatenate([gain] * (NORM_ROWS // SUBLANES), axis=0)

    def apply(r0):
        rows = pl.ds(r0, NORM_ROWS)
        inv = _lane_tile(inv_ref[rows, :], d)
        xn_ref[rows, :] = ((x_ref[rows, :] * inv) * gain).astype(BF16)
    _row_chunks(n_rows, NORM_ROWS, apply)


def _postnorm_residual(x_ref, g_ref, o_ref, scale, ss_ref, inv_ref):
    n_rows, d = x_ref.shape
    _inv_rms_to(o_ref, scale, ss_ref, inv_ref)
    gain = jnp.broadcast_to(g_ref[1:2, :], (SUBLANES, d))

    n_slabs = n_rows // SUBLANES

    def apply(k, o_cur):
        rows = pl.ds(pl.multiple_of(k * SUBLANES, SUBLANES), SUBLANES)
        nxt = pl.multiple_of(jnp.minimum(k + 1, n_slabs - 1) * SUBLANES, SUBLANES)
        o_next = o_ref[pl.ds(nxt, SUBLANES), :]
        inv = _lane_tile(inv_ref[rows, :], d)
        o_ref[rows, :] = x_ref[rows, :] + (o_cur * inv) * gain
        return o_next
    lax.fori_loop(0, n_slabs, apply, o_ref[0:SUBLANES, :], unroll=LOOP_UNROLL)


CAST_BLOCK_BYTES = 4 * 1024 * 1024


def _cast_kernel(w_ref, o_ref):
    o_ref[...] = w_ref[...].astype(BF16)


def _to_bf16(w):
    cols = w.shape[-1]
    w2 = w.reshape(-1, cols)
    rows = w2.shape[0]
    blk = min(rows, CAST_BLOCK_BYTES // (cols * 4) // NORM_ROWS * NORM_ROWS)
    while rows % blk:
        blk -= NORM_ROWS
    out = pl.pallas_call(
        _cast_kernel,
        grid=(rows // blk,),
        in_specs=[pl.BlockSpec((blk, cols), lambda i: (i, 0))],
        out_specs=pl.BlockSpec((blk, cols), lambda i: (i, 0)),
        out_shape=jax.ShapeDtypeStruct((rows, cols), BF16),
        name="to_bf16",
    )(w2)
    return out.reshape(w.shape)


def _compiler_params():
    return pltpu.CompilerParams(
        dimension_semantics=("arbitrary", "arbitrary"),
        vmem_limit_bytes=VMEM_LIMIT_BYTES)


def _ffn_kernel(x_ref, g_ref, wg_ref, wu_ref, wd_ref, o_ref,
                xn_ref, gate_ref, up_ref, h_ref, ss_ref, inv_ref):
    j = pl.program_id(1)

    @pl.when(j == 0)
    def _():
        _prenorm(x_ref, g_ref, xn_ref, ss_ref, inv_ref)
        o_ref[...] = jnp.zeros_like(o_ref)

    xn = xn_ref[...]
    n_halves = FF_TILE // MXU_COLS

    def half_cols(half):
        return slice(half * MXU_COLS, (half + 1) * MXU_COLS)

    def project(half):
        cols = half_cols(half)
        gate_ref[:, cols] = _dot(xn, wg_ref[:, cols])
        up_ref[:, cols] = _dot(xn, wu_ref[:, cols])

    def swiglu(half):
        cols = half_cols(half)
        g = gate_ref[:, cols]
        h_ref[:, cols] = (g * jax.nn.sigmoid(g) * up_ref[:, cols]).astype(BF16)

    def project_down(half):
        o_ref[...] += _dot(h_ref[:, half_cols(half)], wd_ref[half_cols(half), :])

    project(0)
    for half in range(n_halves):
        if half + 1 < n_halves:
            project(half + 1)
        swiglu(half)
        project_down(half)

    @pl.when(j == pl.num_programs(1) - 1)
    def _():
        _postnorm_residual(x_ref, g_ref, o_ref, 0.5, ss_ref, inv_ref)


def _ffn(x, gains, wg, wu, wd, layer, row_tile):
    m, d = x.shape
    grid = (m // row_tile, wg.shape[-1] // FF_TILE)
    return pl.pallas_call(
        _ffn_kernel,
        grid=grid,
        in_specs=[
            pl.BlockSpec((row_tile, d), lambda i, j: (i, 0)),
            pl.BlockSpec((2, d), lambda i, j: (0, 0)),
            pl.BlockSpec((None, d, FF_TILE), lambda i, j: (layer, 0, j)),
            pl.BlockSpec((None, d, FF_TILE), lambda i, j: (layer, 0, j)),
            pl.BlockSpec((None, FF_TILE, d), lambda i, j: (layer, j, 0)),
        ],
        out_specs=pl.BlockSpec((row_tile, d), lambda i, j: (i, 0)),
        out_shape=jax.ShapeDtypeStruct((m, d), F32),
        scratch_shapes=[
            pltpu.VMEM((row_tile, d), BF16),
            pltpu.VMEM((row_tile, FF_TILE), F32),
            pltpu.VMEM((row_tile, FF_TILE), F32),
            pltpu.VMEM((row_tile, FF_TILE), BF16),
            pltpu.VMEM((row_tile, LANES), F32),
            pltpu.VMEM((row_tile, LANES), F32),
        ],
        compiler_params=_compiler_params(),
        name="ffn",
    )(x, gains, wg, wu, wd)


def _neg_softplus_neg(lam):
    return -(jnp.maximum(-lam, 0.0) + jnp.log1p(jnp.exp(-jnp.abs(lam))))


def _rglru_coeffs(r_pre, i_pre, u, log_sig_lam):
    r = jax.nn.sigmoid(r_pre)
    ig = jax.nn.sigmoid(i_pre)
    log_a = RG_C * r * log_sig_lam
    a = jnp.exp(log_a)
    one_minus_a2 = -jnp.tanh(log_a) * (a * a + 1.0)
    return a, jnp.sqrt(one_minus_a2) * (ig * u)


def _conv_taps(stage_ref, w_ref, r0, rows, width, cols=slice(None)):
    hw = width - 1
    out = stage_ref[SUBLANES + r0:SUBLANES + r0 + rows, cols] * w_ref[hw:hw + 1, cols]
    for q in range(hw):
        lo = SUBLANES - hw + q + r0
        out = out + stage_ref[lo:lo + rows, cols] * w_ref[q:q + 1, cols]
    return out


def _rglru_main_kernel(x_ref, g_ref, wgate_ref, wu_ref, cw_ref, cb_ref, gw_ref, gb_ref,
                       lam_ref, wo_ref, h0_ref, hist0_ref,
                       o_ref, hlast_ref, histlast_ref,
                       xn_ref, state_ref, gate_ref, stage_ref, u_ref, gm_ref, y_ref,
                       ss_ref, inv_ref):
    i = pl.program_id(0)
    c = pl.program_id(1)
    n_rows = x_ref.shape[0]
    hw = CONV_A - 1
    n_blocks = CH_TILE // RG_BLOCK
    block_cols = [slice(nb * RG_BLOCK, (nb + 1) * RG_BLOCK) for nb in range(n_blocks)]

    @pl.when(c == 0)
    def _():
        _prenorm(x_ref, g_ref, xn_ref, ss_ref, inv_ref)
        o_ref[...] = jnp.zeros_like(o_ref)

    @pl.when(i == 0)
    def _():
        state_ref[c, 0:1, :] = h0_ref[0:1, :]
        state_ref[c, 1:1 + hw, :] = hist0_ref[0]

    xn = xn_ref[...]
    log_sig_lam = _neg_softplus_neg(lam_ref[...])
    sub = lax.broadcasted_iota(jnp.int32, (SUBLANES, RG_BLOCK), 0)
    n_halves = CH_TILE // MXU_COLS
    blocks_per_half = n_blocks // n_halves

    def half_cols(half):
        return slice(half * MXU_COLS, (half + 1) * MXU_COLS)

    def project_gate(half):
        gate_ref[:, half_cols(half)] = _dot(xn, wgate_ref[:, half_cols(half)])

    def project_u(half):
        stage_ref[SUBLANES:SUBLANES + n_rows, half_cols(half)] = _dot(xn, wu_ref[:, half_cols(half)])

    def conv_and_gate_matmuls(half):
        cols = half_cols(half)
        stage_ref[SUBLANES - hw:SUBLANES, cols] = state_ref[c, 1:1 + hw, cols]
        new_hist = stage_ref[SUBLANES + n_rows - hw:SUBLANES + n_rows, cols]
        state_ref[c, 1:1 + hw, cols] = new_hist
        histlast_ref[0, :, cols] = new_hist
        for r0 in range(0, n_rows, CONV_ROWS):
            u_ref[r0:r0 + CONV_ROWS, cols] = (
                _conv_taps(stage_ref, cw_ref, r0, CONV_ROWS, CONV_A, cols) + cb_ref[:, cols])
        for nb in range(half * blocks_per_half, (half + 1) * blocks_per_half):
            ub = u_ref[:, block_cols[nb]].astype(BF16)
            gm_ref[:, 2 * nb * RG_BLOCK:2 * (nb + 1) * RG_BLOCK] = _dot(ub, gw_ref[nb])

    def slab_scan(a, b, h_prev):
        for dist in (1, 2, 4):
            a_up = jnp.where(sub >= dist, pltpu.roll(a, dist, 0), 1.0)
            b_up = jnp.where(sub >= dist, pltpu.roll(b, dist, 0), 0.0)
            b = a * b_up + b
            a = a * a_up
        return b + a * h_prev

    def recurrence(half):
        blocks = range(half * blocks_per_half, (half + 1) * blocks_per_half)
        h_prev = {nb: state_ref[c, 0:1, block_cols[nb]] for nb in blocks}
        for r0 in range(0, n_rows, NORM_ROWS):
            rows = slice(r0, r0 + NORM_ROWS)
            for nb in blocks:
                cols = block_cols[nb]
                g0 = 2 * nb * RG_BLOCK
                bias = gb_ref[nb]
                a, b = _rglru_coeffs(
                    gm_ref[rows, g0:g0 + RG_BLOCK] + bias[:, :RG_BLOCK],
                    gm_ref[rows, g0 + RG_BLOCK:g0 + 2 * RG_BLOCK] + bias[:, RG_BLOCK:],
                    u_ref[rows, cols], log_sig_lam[:, cols])
                h_lo = slab_scan(a[:SUBLANES], b[:SUBLANES], h_prev[nb])
                h_hi = slab_scan(a[SUBLANES:], b[SUBLANES:], h_lo[SUBLANES - 1:SUBLANES, :])
                h = jnp.concatenate([h_lo, h_hi], axis=0)
                y_ref[rows, cols] = (jax.nn.gelu(gate_ref[rows, cols]) * h).astype(BF16)
                h_prev[nb] = h_hi[SUBLANES - 1:SUBLANES, :]
        for nb in blocks:
            state_ref[c, 0:1, block_cols[nb]] = h_prev[nb]
            hlast_ref[0:1, block_cols[nb]] = h_prev[nb]

    def project_out(half):
        o_ref[...] += _dot(y_ref[:, half_cols(half)], wo_ref[half_cols(half), :])

    project_gate(0)
    project_u(0)
    for half in range(n_halves):
        if half + 1 < n_halves:
            project_gate(half + 1)
        conv_and_gate_matmuls(half)
        if half + 1 < n_halves:
            project_u(half + 1)
        if half > 0:
            project_out(half - 1)
        recurrence(half)
    project_out(n_halves - 1)

    @pl.when(c == pl.num_programs(1) - 1)
    def _():
        _postnorm_residual(x_ref, g_ref, o_ref, 1.0, ss_ref, inv_ref)


def _rglru(x, gains, w_in, conv_w, conv_b, gate_w, gate_b, lam, w_out, h0, hist0,
           layer, row_tile, segments):
    m, d = x.shape
    d_rnn = w_out.shape[-2]
    nc = d_rnn // CH_TILE
    nb = CH_TILE // RG_BLOCK
    hw = CONV_A - 1
    n_seq = 1 if segments is None else len(segments)
    grid = (m // row_tile, nc)
    tile_f32 = pltpu.VMEM((row_tile, CH_TILE), F32)
    norm_scratch = [pltpu.VMEM((row_tile, LANES), F32)] * 2
    if segments is None:
        kernel = _rglru_main_kernel
        scratch = [
            pltpu.VMEM((row_tile, d), BF16),
            pltpu.VMEM((nc, SUBLANES, CH_TILE), F32),
            tile_f32,
            pltpu.VMEM((row_tile + SUBLANES, CH_TILE), F32),
            tile_f32,
            pltpu.VMEM((row_tile, 2 * CH_TILE), F32),
            pltpu.VMEM((row_tile, CH_TILE), BF16),
        ] + norm_scratch
    else:
        kernel = functools.partial(_rglru_small_kernel, tuple(segments))
        scratch = [
            pltpu.VMEM((row_tile, d), BF16),
            pltpu.VMEM((row_tile + SUBLANES, CH_TILE), F32),
            tile_f32, tile_f32, tile_f32, tile_f32,
        ] + norm_scratch
    return pl.pallas_call(
        kernel,
        grid=grid,
        in_specs=[
            pl.BlockSpec((row_tile, d), lambda i, c: (i, 0)),
            pl.BlockSpec((2, d), lambda i, c: (0, 0)),
            pl.BlockSpec((None, d, CH_TILE), lambda i, c: (layer, 0, c)),
            pl.BlockSpec((None, d, CH_TILE), lambda i, c: (layer, 0, nc + c)),
            pl.BlockSpec((None, CONV_A, CH_TILE), lambda i, c: (layer, 0, c)),
            pl.BlockSpec((None, 1, CH_TILE), lambda i, c: (layer, 0, c)),
            pl.BlockSpec((None, nb, RG_BLOCK, 2 * RG_BLOCK), lambda i, c: (layer, c, 0, 0)),
            pl.BlockSpec((None, nb, 1, 2 * RG_BLOCK), lambda i, c: (layer, c, 0, 0)),
            pl.BlockSpec((None, 1, CH_TILE), lambda i, c: (layer, 0, c)),
            pl.BlockSpec((None, CH_TILE, d), lambda i, c: (layer, c, 0)),
            pl.BlockSpec((n_seq, CH_TILE), lambda i, c: (0, c)),
            pl.BlockSpec((n_seq, hw, CH_TILE), lambda i, c: (0, 0, c)),
        ],
        out_specs=[
            pl.BlockSpec((row_tile, d), lambda i, c: (i, 0)),
            pl.BlockSpec((None, n_seq, CH_TILE), lambda i, c: (i, 0, c)),
            pl.BlockSpec((None, n_seq, hw, CH_TILE), lambda i, c: (i, 0, 0, c)),
        ],
        out_shape=[
            jax.ShapeDtypeStruct((m, d), F32),
            jax.ShapeDtypeStruct((grid[0], n_seq, d_rnn), F32),
            jax.ShapeDtypeStruct((grid[0], n_seq, hw, d_rnn), F32),
        ],
        scratch_shapes=scratch,
        compiler_params=_compiler_params(),
        name="rglru",
    )(x, gains, w_in, w_in, conv_w, conv_b, gate_w, gate_b, lam, w_out, h0, hist0)


def _segment_conv(stage_ref, seq, hist, w_ref, width):
    n = seq.shape[0]
    hw = width - 1
    stage_ref[SUBLANES - hw:SUBLANES, :] = hist
    stage_ref[SUBLANES:SUBLANES + n, :] = seq
    return (_conv_taps(stage_ref, w_ref, 0, n, width),
            stage_ref[SUBLANES + n - hw:SUBLANES + n, :])


def _rglru_small_kernel(segments,
                        x_ref, g_ref, wgate_ref, wu_ref, cw_ref, cb_ref, gw_ref, gb_ref,
                        lam_ref, wo_ref, h0_ref, hist0_ref,
                        o_ref, hlast_ref, histlast_ref,
                        xn_ref, stage_ref, u_ref, a_ref, b_ref, h_ref, ss_ref, inv_ref):
    c = pl.program_id(1)

    @pl.when(c == 0)
    def _():
        _prenorm(x_ref, g_ref, xn_ref, ss_ref, inv_ref)
        o_ref[...] = jnp.zeros_like(o_ref)

    xn = xn_ref[...]
    gate = _dot(xn, wgate_ref[...])
    u_pre = _dot(xn, wu_ref[...])

    for s, (r0, n) in enumerate(segments):
        conv, new_hist = _segment_conv(stage_ref, u_pre[r0:r0 + n, :], hist0_ref[s], cw_ref, CONV_A)
        u_ref[r0:r0 + n, :] = conv + cb_ref[...]
        histlast_ref[s] = new_hist

    log_sig_lam = _neg_softplus_neg(lam_ref[...])
    for nb in range(CH_TILE // RG_BLOCK):
        cols = slice(nb * RG_BLOCK, (nb + 1) * RG_BLOCK)
        ub = u_ref[:, cols]
        gm = _dot(ub.astype(BF16), gw_ref[nb]) + gb_ref[nb]
        a, b = _rglru_coeffs(gm[:, :RG_BLOCK], gm[:, RG_BLOCK:], ub, log_sig_lam[:, cols])
        a_ref[:, cols] = a
        b_ref[:, cols] = b

    for s, (r0, n) in enumerate(segments):
        def step(t, h):
            h = a_ref[pl.ds(t, 1), :] * h + b_ref[pl.ds(t, 1), :]
            h_ref[pl.ds(t, 1), :] = h
            return h
        hlast_ref[s:s + 1, :] = lax.fori_loop(r0, r0 + n, step, h0_ref[s:s + 1, :])

    y = (jax.nn.gelu(gate) * h_ref[...]).astype(BF16)
    o_ref[...] += _dot(y, wo_ref[...])

    @pl.when(c == pl.num_programs(1) - 1)
    def _():
        _postnorm_residual(x_ref, g_ref, o_ref, 1.0, ss_ref, inv_ref)


def _shortconv_small_kernel(segments,
                            x_ref, g_ref, wb_ref, wc_ref, wv_ref, cw_ref, wo_ref, hist0_ref,
                            o_ref, histlast_ref,
                            xn_ref, stage_ref, y_ref, ss_ref, inv_ref):
    c = pl.program_id(1)

    @pl.when(c == 0)
    def _():
        _prenorm(x_ref, g_ref, xn_ref, ss_ref, inv_ref)
        o_ref[...] = jnp.zeros_like(o_ref)

    xn = xn_ref[...]
    gb = _dot(xn, wb_ref[...])
    cv = _dot(xn, wc_ref[...]) * _dot(xn, wv_ref[...])

    for s, (r0, n) in enumerate(segments):
        z, new_hist = _segment_conv(stage_ref, cv[r0:r0 + n, :], hist0_ref[s], cw_ref, CONV_B)
        y_ref[r0:r0 + n, :] = (gb[r0:r0 + n, :] * z).astype(BF16)
        histlast_ref[s] = new_hist

    o_ref[...] += _dot(y_ref[...], wo_ref[...])

    @pl.when(c == pl.num_programs(1) - 1)
    def _():
        _postnorm_residual(x_ref, g_ref, o_ref, 1.0, ss_ref, inv_ref)


def _shortconv_main_kernel(x_ref, g_ref, wb_ref, wc_ref, wv_ref, cw_ref, wo_ref, hist0_ref,
                           o_ref, histlast_ref,
                           xn_ref, state_ref, b_ref, c_ref, stage_ref, y_ref, ss_ref, inv_ref):
    i = pl.program_id(0)
    c = pl.program_id(1)
    n_rows = x_ref.shape[0]
    hw = CONV_B - 1

    @pl.when(c == 0)
    def _():
        _prenorm(x_ref, g_ref, xn_ref, ss_ref, inv_ref)
        o_ref[...] = jnp.zeros_like(o_ref)

    @pl.when(i == 0)
    def _():
        state_ref[c, 0:hw, :] = hist0_ref[0]

    xn = xn_ref[...]
    n_halves = CH_TILE // MXU_COLS

    def half_cols(half):
        return slice(half * MXU_COLS, (half + 1) * MXU_COLS)

    def project(half):
        cols = half_cols(half)
        b_ref[:, cols] = _dot(xn, wb_ref[:, cols])
        c_ref[:, cols] = _dot(xn, wc_ref[:, cols])
        stage_ref[SUBLANES:SUBLANES + n_rows, cols] = _dot(xn, wv_ref[:, cols])

    def conv_and_gate(half):
        cols = half_cols(half)
        for r0 in range(0, n_rows, CONV_ROWS):
            rows = slice(SUBLANES + r0, SUBLANES + r0 + CONV_ROWS)
            stage_ref[rows, cols] = stage_ref[rows, cols] * c_ref[r0:r0 + CONV_ROWS, cols]
        stage_ref[SUBLANES - hw:SUBLANES, cols] = state_ref[c, 0:hw, cols]
        new_hist = stage_ref[SUBLANES + n_rows - hw:SUBLANES + n_rows, cols]
        state_ref[c, 0:hw, cols] = new_hist
        histlast_ref[0, :, cols] = new_hist
        for r0 in range(0, n_rows, CONV_ROWS):
            z = _conv_taps(stage_ref, cw_ref, r0, CONV_ROWS, CONV_B, cols)
            y_ref[r0:r0 + CONV_ROWS, cols] = (b_ref[r0:r0 + CONV_ROWS, cols] * z).astype(BF16)

    def project_out(half):
        o_ref[...] += _dot(y_ref[:, half_cols(half)], wo_ref[half_cols(half), :])

    project(0)
    for half in range(n_halves):
        if half + 1 < n_halves:
            project(half + 1)
        conv_and_gate(half)
        project_out(half)

    @pl.when(c == pl.num_programs(1) - 1)
    def _():
        _postnorm_residual(x_ref, g_ref, o_ref, 1.0, ss_ref, inv_ref)


def _shortconv(x, gains, w_in, conv_w, w_out, hist0, layer, row_tile, segments):
    m, d = x.shape
    d_conv = w_out.shape[-2]
    nc = d_conv // CH_TILE
    hw = CONV_B - 1
    n_seq = 1 if segments is None else len(segments)
    grid = (m // row_tile, nc)
    tile_f32 = pltpu.VMEM((row_tile, CH_TILE), F32)
    norm_scratch = [pltpu.VMEM((row_tile, LANES), F32)] * 2
    if segments is None:
        kernel = _shortconv_main_kernel
        scratch = [
            pltpu.VMEM((row_tile, d), BF16),
            pltpu.VMEM((nc, SUBLANES, CH_TILE), F32),
            tile_f32, tile_f32,
            pltpu.VMEM((row_tile + SUBLANES, CH_TILE), F32),
            pltpu.VMEM((row_tile, CH_TILE), BF16),
        ] + norm_scratch
    else:
        kernel = functools.partial(_shortconv_small_kernel, tuple(segments))
        scratch = [
            pltpu.VMEM((row_tile, d), BF16),
            pltpu.VMEM((row_tile + SUBLANES, CH_TILE), F32),
            pltpu.VMEM((row_tile, CH_TILE), BF16),
        ] + norm_scratch
    return pl.pallas_call(
        kernel,
        grid=grid,
        in_specs=[
            pl.BlockSpec((row_tile, d), lambda i, c: (i, 0)),
            pl.BlockSpec((2, d), lambda i, c: (0, 0)),
            pl.BlockSpec((None, d, CH_TILE), lambda i, c: (layer, 0, c)),
            pl.BlockSpec((None, d, CH_TILE), lambda i, c: (layer, 0, nc + c)),
            pl.BlockSpec((None, d, CH_TILE), lambda i, c: (layer, 0, 2 * nc + c)),
            pl.BlockSpec((None, CONV_B, CH_TILE), lambda i, c: (layer, 0, c)),
            pl.BlockSpec((None, CH_TILE, d), lambda i, c: (layer, c, 0)),
            pl.BlockSpec((n_seq, hw, CH_TILE), lambda i, c: (0, 0, c)),
        ],
        out_specs=[
            pl.BlockSpec((row_tile, d), lambda i, c: (i, 0)),
            pl.BlockSpec((None, n_seq, hw, CH_TILE), lambda i, c: (i, 0, 0, c)),
        ],
        out_shape=[
            jax.ShapeDtypeStruct((m, d), F32),
            jax.ShapeDtypeStruct((grid[0], n_seq, hw, d_conv), F32),
        ],
        scratch_shapes=scratch,
        compiler_params=_compiler_params(),
        name="shortconv",
    )(x, gains, w_in, w_in, w_in, conv_w, w_out, hist0)


def kernel(x_prompt, x_sample, state_a_h, cache_a_conv, cache_b_conv, meta_tokens, norm_g, ffn1_wg, ffn1_wu, ffn1_wd, ffn2_wg, ffn2_wu, ffn2_wd, a_w_in, a_conv_w, a_conv_b, a_gate_w, a_gate_b, a_lambda, a_w_out, b_w_in, b_conv_w, b_w_out):
    depth = norm_g.shape[0]
    n_prompt, seq, d = x_prompt.shape
    n_dec, dec_seq, _ = x_sample.shape
    assert n_prompt == 1, "prompt rows are treated as one causal sequence"
    assert seq % ROW_TILE == 0 and seq % FFN_ROW_TILE == 0

    x_main = x_prompt.reshape(seq, d)
    x_small = jnp.concatenate(
        [meta_tokens.astype(F32), x_sample.reshape(n_dec * dec_seq, d)], axis=0)
    m_small = x_small.shape[0]
    small_segments = [(0, N_META)] + [(N_META + b * dec_seq, dec_seq) for b in range(n_dec)]

    bf = _to_bf16

    ffn_w = [(bf(ffn1_wg), bf(ffn1_wu), bf(ffn1_wd)), (bf(ffn2_wg), bf(ffn2_wu), bf(ffn2_wd))]
    a_w_in_b, a_w_out_b, a_gate_w_b = bf(a_w_in), bf(a_w_out), bf(a_gate_w)
    b_w_in_b, b_w_out_b = bf(b_w_in), bf(b_w_out)
    a_conv_b3 = a_conv_b[:, None, :]
    a_lambda3 = a_lambda[:, None, :]
    a_gate_b4 = a_gate_b[:, :, None, :]

    p_h, p_ha, p_hb, s_h, s_ha, s_hb = [], [], [], [], [], []
    for layer in range(depth):
        g = norm_g[layer]
        j = layer // 2

        x_main = _ffn(x_main, g[0:2], *ffn_w[0], layer, FFN_ROW_TILE)
        x_small = _ffn(x_small, g[0:2], *ffn_w[0], layer, m_small)

        if layer % 2 == 0:
            h0 = jnp.concatenate([jnp.zeros((1, state_a_h.shape[-1]), F32), state_a_h[j]], axis=0)
            hist0 = jnp.concatenate(
                [jnp.zeros((1,) + cache_a_conv.shape[2:], F32), cache_a_conv[j]], axis=0)
            a_args = (a_w_in_b, a_conv_w, a_conv_b3, a_gate_w_b, a_gate_b4, a_lambda3, a_w_out_b)
            x_small, h_s, hist_s = _rglru(x_small, g[2:4], *a_args, h0, hist0,
                                          j, m_small, small_segments)
            h_s, hist_s = h_s[0], hist_s[0]
            x_main, h_m, hist_m = _rglru(x_main, g[2:4], *a_args, h_s[0:1], hist_s[0:1],
                                         j, ROW_TILE, None)
            p_h.append(h_m[-1])
            p_ha.append(hist_m[-1])
            s_h.append(h_s[1:])
            s_ha.append(hist_s[1:])
        else:
            hist0 = jnp.concatenate(
                [jnp.zeros((1,) + cache_b_conv.shape[2:], F32), cache_b_conv[j]], axis=0)
            b_args = (b_w_in_b, b_conv_w, b_w_out_b)
            x_small, hist_s = _shortconv(x_small, g[2:4], *b_args, hist0,
                                         j, m_small, small_segments)
            hist_s = hist_s[0]
            x_main, hist_m = _shortconv(x_main, g[2:4], *b_args, hist_s[0:1],
                                        j, ROW_TILE, None)
            p_hb.append(hist_m[-1])
            s_hb.append(hist_s[1:])

        x_main = _ffn(x_main, g[4:6], *ffn_w[1], layer, FFN_ROW_TILE)
        x_small = _ffn(x_small, g[4:6], *ffn_w[1], layer, m_small)

    y_prompt = x_main.reshape(1, seq, d)
    y_sample = x_small[N_META:].reshape(n_dec, dec_seq, d)
    return (y_prompt, y_sample,
            jnp.stack(p_h), jnp.stack(p_ha), jnp.stack(p_hb),
            jnp.stack(s_h), jnp.stack(s_ha), jnp.stack(s_hb))
```

```python
import functools

import jax
import jax.numpy as jnp
from jax import lax
from jax.experimental import pallas as pl
from jax.experimental.pallas import tpu as pltpu

F32 = jnp.float32
BF16 = jnp.bfloat16

EPS = 1e-6
RG_C = 8.0
RG_BLOCK = 128
N_META = 16
CONV_A = 4
CONV_B = 3

LANES = 128
SUBLANES = 8
MXU_COLS = 256
VMEM_LIMIT_BYTES = 60 * 1024 * 1024

ROW_TILE = 512
FFN_ROW_TILE = 1024
FF_TILE = 512
CH_TILE = 512
NORM_ROWS = 16
CONV_ROWS = 64
LOOP_UNROLL = 8


def _dot(a, b):
    return jnp.dot(a, b, preferred_element_type=F32)


def _row_chunks(n_rows, rows_per_chunk, body, unroll=LOOP_UNROLL):
    def step(k, _):
        body(pl.multiple_of(k * rows_per_chunk, rows_per_chunk))
        return 0
    lax.fori_loop(0, n_rows // rows_per_chunk, step, 0, unroll=unroll)


def _lane_tile(v, d):
    return jnp.concatenate([v] * (d // LANES), axis=-1)


def _inv_rms_to(src_ref, scale, ss_ref, inv_ref):
    n_rows, d = src_ref.shape

    def partial(r0):
        x = src_ref[pl.ds(r0, SUBLANES), :]
        x2 = x * x
        acc = x2[:, 0:LANES]
        for q in range(1, d // LANES):
            acc = acc + x2[:, q * LANES:(q + 1) * LANES]
        ss_ref[pl.ds(r0, SUBLANES), :] = acc
    _row_chunks(n_rows, SUBLANES, partial)
    tot = jnp.sum(ss_ref[...], axis=-1, keepdims=True)
    inv = scale * lax.rsqrt(tot * (1.0 / d) + EPS)
    inv_ref[...] = jnp.broadcast_to(inv, inv_ref.shape)


def _prenorm(x_ref, g_ref, xn_ref, ss_ref, inv_ref):
    n_rows, d = x_ref.shape
    _inv_rms_to(x_ref, 1.0, ss_ref, inv_ref)
    gain = jnp.broadcast_to(g_ref[0:1, :], (SUBLANES, d))
    gain = jnp.concatenate([gain] * (NORM_ROWS // SUBLANES), axis=0)

    def apply(r0):
        rows = pl.ds(r0, NORM_ROWS)
        inv = _lane_tile(inv_ref[rows, :], d)
        xn_ref[rows, :] = ((x_ref[rows, :] * inv) * gain).astype(BF16)
    _row_chunks(n_rows, NORM_ROWS, apply)


def _postnorm_residual(x_ref, g_ref, o_ref, scale, ss_ref, inv_ref):
    n_rows, d = x_ref.shape
    _inv_rms_to(o_ref, scale, ss_ref, inv_ref)
    gain = jnp.broadcast_to(g_ref[1:2, :], (SUBLANES, d))

    n_slabs = n_rows // SUBLANES

    def apply(k, o_cur):
        rows = pl.ds(pl.multiple_of(k * SUBLANES, SUBLANES), SUBLANES)
        nxt = pl.multiple_of(jnp.minimum(k + 1, n_slabs - 1) * SUBLANES, SUBLANES)
        o_next = o_ref[pl.ds(nxt, SUBLANES), :]
        inv = _lane_tile(inv_ref[rows, :], d)
        o_ref[rows, :] = x_ref[rows, :] + (o_cur * inv) * gain
        return o_next
    lax.fori_loop(0, n_slabs, apply, o_ref[0:SUBLANES, :], unroll=LOOP_UNROLL)


def _compiler_params():
    return pltpu.CompilerParams(
        dimension_semantics=("arbitrary", "arbitrary"),
        vmem_limit_bytes=VMEM_LIMIT_BYTES)


def _weight(w_ref, copy_ref, idx):
    w = w_ref[idx]
    if copy_ref is None:
        return w
    w = w.astype(BF16)
    copy_ref[idx] = w
    return w


def _ffn_kernel(emit_bf16, *refs):
    x_ref, g_ref, wg_ref, wu_ref, wd_ref, o_ref = refs[:6]
    wg_copy, wu_copy, wd_copy = refs[6:9] if emit_bf16 else (None, None, None)
    xn_ref, gate_ref, up_ref, h_ref, ss_ref, inv_ref = refs[-6:]
    j = pl.program_id(1)

    @pl.when(j == 0)
    def _():
        _prenorm(x_ref, g_ref, xn_ref, ss_ref, inv_ref)
        o_ref[...] = jnp.zeros_like(o_ref)

    xn = xn_ref[...]
    n_halves = FF_TILE // MXU_COLS

    def half_cols(half):
        return slice(half * MXU_COLS, (half + 1) * MXU_COLS)

    def project(half):
        cols = half_cols(half)
        gate_ref[:, cols] = _dot(xn, _weight(wg_ref, wg_copy, (slice(None), cols)))
        up_ref[:, cols] = _dot(xn, _weight(wu_ref, wu_copy, (slice(None), cols)))

    def swiglu(half):
        cols = half_cols(half)
        g = gate_ref[:, cols]
        h_ref[:, cols] = (g * jax.nn.sigmoid(g) * up_ref[:, cols]).astype(BF16)

    def project_down(half):
        cols = half_cols(half)
        o_ref[...] += _dot(h_ref[:, cols], _weight(wd_ref, wd_copy, (cols, slice(None))))

    project(0)
    for half in range(n_halves):
        if half + 1 < n_halves:
            project(half + 1)
        swiglu(half)
        project_down(half)

    @pl.when(j == pl.num_programs(1) - 1)
    def _():
        _postnorm_residual(x_ref, g_ref, o_ref, 0.5, ss_ref, inv_ref)


def _ffn(x, gains, wg, wu, wd, row_tile, layer=None):
    m, d = x.shape
    d_ff = wg.shape[-1]
    grid = (m // row_tile, d_ff // FF_TILE)
    emit = layer is not None
    stack_specs = [
        pl.BlockSpec((None, d, FF_TILE), lambda i, j: (layer, 0, j)),
        pl.BlockSpec((None, d, FF_TILE), lambda i, j: (layer, 0, j)),
        pl.BlockSpec((None, FF_TILE, d), lambda i, j: (layer, j, 0)),
    ]
    flat_specs = [
        pl.BlockSpec((d, FF_TILE), lambda i, j: (0, j)),
        pl.BlockSpec((d, FF_TILE), lambda i, j: (0, j)),
        pl.BlockSpec((FF_TILE, d), lambda i, j: (j, 0)),
    ]
    flat_shapes = [jax.ShapeDtypeStruct((d, d_ff), BF16), jax.ShapeDtypeStruct((d, d_ff), BF16),
                   jax.ShapeDtypeStruct((d_ff, d), BF16)]
    x_spec = pl.BlockSpec((row_tile, d), lambda i, j: (i, 0))
    out = pl.pallas_call(
        functools.partial(_ffn_kernel, emit),
        grid=grid,
        in_specs=[x_spec, pl.BlockSpec((2, d), lambda i, j: (0, 0))]
        + (stack_specs if emit else flat_specs),
        out_specs=[x_spec] + (flat_specs if emit else []),
        out_shape=[jax.ShapeDtypeStruct((m, d), F32)] + (flat_shapes if emit else []),
        scratch_shapes=[
            pltpu.VMEM((row_tile, d), BF16),
            pltpu.VMEM((row_tile, FF_TILE), F32),
            pltpu.VMEM((row_tile, FF_TILE), F32),
            pltpu.VMEM((row_tile, FF_TILE), BF16),
            pltpu.VMEM((row_tile, LANES), F32),
            pltpu.VMEM((row_tile, LANES), F32),
        ],
        compiler_params=_compiler_params(),
        name="ffn",
    )(x, gains, wg, wu, wd)
    return out if emit else out[0]


def _neg_softplus_neg(lam):
    return -(jnp.maximum(-lam, 0.0) + jnp.log1p(jnp.exp(-jnp.abs(lam))))


def _rglru_coeffs(r_pre, i_pre, u, log_sig_lam):
    r = jax.nn.sigmoid(r_pre)
    ig = jax.nn.sigmoid(i_pre)
    log_a = RG_C * r * log_sig_lam
    a = jnp.exp(log_a)
    one_minus_a2 = -jnp.tanh(log_a) * (a * a + 1.0)
    return a, jnp.sqrt(one_minus_a2) * (ig * u)


def _conv_taps(stage_ref, w_ref, r0, rows, width, cols=slice(None)):
    hw = width - 1
    out = stage_ref[SUBLANES + r0:SUBLANES + r0 + rows, cols] * w_ref[hw:hw + 1, cols]
    for q in range(hw):
        lo = SUBLANES - hw + q + r0
        out = out + stage_ref[lo:lo + rows, cols] * w_ref[q:q + 1, cols]
    return out


def _rglru_main_kernel(x_ref, g_ref, wgate_ref, wu_ref, gw_ref, wo_ref, cw_ref, cb_ref, gb_ref,
                       lam_ref, h0_ref, hist0_ref,
                       o_ref, hlast_ref, histlast_ref,
                       xn_ref, state_ref, gate_ref, stage_ref, u_ref, gm_ref, y_ref,
                       ss_ref, inv_ref):
    i = pl.program_id(0)
    c = pl.program_id(1)
    n_rows = x_ref.shape[0]
    hw = CONV_A - 1
    n_blocks = CH_TILE // RG_BLOCK
    block_cols = [slice(nb * RG_BLOCK, (nb + 1) * RG_BLOCK) for nb in range(n_blocks)]

    @pl.when(c == 0)
    def _():
        _prenorm(x_ref, g_ref, xn_ref, ss_ref, inv_ref)
        o_ref[...] = jnp.zeros_like(o_ref)

    @pl.when(i == 0)
    def _():
        state_ref[c, 0:1, :] = h0_ref[0:1, :]
        state_ref[c, 1:1 + hw, :] = hist0_ref[0]

    xn = xn_ref[...]
    log_sig_lam = _neg_softplus_neg(lam_ref[...])
    sub = lax.broadcasted_iota(jnp.int32, (SUBLANES, RG_BLOCK), 0)
    n_halves = CH_TILE // MXU_COLS
    blocks_per_half = n_blocks // n_halves

    def half_cols(half):
        return slice(half * MXU_COLS, (half + 1) * MXU_COLS)

    def project_gate(half):
        gate_ref[:, half_cols(half)] = _dot(xn, wgate_ref[:, half_cols(half)])

    def project_u(half):
        stage_ref[SUBLANES:SUBLANES + n_rows, half_cols(half)] = _dot(xn, wu_ref[:, half_cols(half)])

    def conv_and_gate_matmuls(half):
        cols = half_cols(half)
        stage_ref[SUBLANES - hw:SUBLANES, cols] = state_ref[c, 1:1 + hw, cols]
        new_hist = stage_ref[SUBLANES + n_rows - hw:SUBLANES + n_rows, cols]
        state_ref[c, 1:1 + hw, cols] = new_hist
        histlast_ref[0, :, cols] = new_hist
        for r0 in range(0, n_rows, CONV_ROWS):
            u_ref[r0:r0 + CONV_ROWS, cols] = (
                _conv_taps(stage_ref, cw_ref, r0, CONV_ROWS, CONV_A, cols) + cb_ref[:, cols])
        for nb in range(half * blocks_per_half, (half + 1) * blocks_per_half):
            ub = u_ref[:, block_cols[nb]].astype(BF16)
            gm_ref[:, 2 * nb * RG_BLOCK:2 * (nb + 1) * RG_BLOCK] = _dot(ub, gw_ref[nb])

    def slab_scan(a, b, h_prev):
        for dist in (1, 2, 4):
            a_up = jnp.where(sub >= dist, pltpu.roll(a, dist, 0), 1.0)
            b_up = jnp.where(sub >= dist, pltpu.roll(b, dist, 0), 0.0)
            b = a * b_up + b
            a = a * a_up
        return b + a * h_prev

    def recurrence(half):
        blocks = range(half * blocks_per_half, (half + 1) * blocks_per_half)
        h_prev = {nb: state_ref[c, 0:1, block_cols[nb]] for nb in blocks}
        for r0 in range(0, n_rows, NORM_ROWS):
            rows = slice(r0, r0 + NORM_ROWS)
            for nb in blocks:
                cols = block_cols[nb]
                g0 = 2 * nb * RG_BLOCK
                bias = gb_ref[nb]
                a, b = _rglru_coeffs(
                    gm_ref[rows, g0:g0 + RG_BLOCK] + bias[:, :RG_BLOCK],
                    gm_ref[rows, g0 + RG_BLOCK:g0 + 2 * RG_BLOCK] + bias[:, RG_BLOCK:],
                    u_ref[rows, cols], log_sig_lam[:, cols])
                h_lo = slab_scan(a[:SUBLANES], b[:SUBLANES], h_prev[nb])
                h_hi = slab_scan(a[SUBLANES:], b[SUBLANES:], h_lo[SUBLANES - 1:SUBLANES, :])
                h = jnp.concatenate([h_lo, h_hi], axis=0)
                y_ref[rows, cols] = (jax.nn.gelu(gate_ref[rows, cols]) * h).astype(BF16)
                h_prev[nb] = h_hi[SUBLANES - 1:SUBLANES, :]
        for nb in blocks:
            state_ref[c, 0:1, block_cols[nb]] = h_prev[nb]
            hlast_ref[0:1, block_cols[nb]] = h_prev[nb]

    def project_out(half):
        o_ref[...] += _dot(y_ref[:, half_cols(half)], wo_ref[half_cols(half), :])

    project_gate(0)
    project_u(0)
    for half in range(n_halves):
        if half + 1 < n_halves:
            project_gate(half + 1)
        conv_and_gate_matmuls(half)
        if half + 1 < n_halves:
            project_u(half + 1)
        if half > 0:
            project_out(half - 1)
        recurrence(half)
    project_out(n_halves - 1)

    @pl.when(c == pl.num_programs(1) - 1)
    def _():
        _postnorm_residual(x_ref, g_ref, o_ref, 1.0, ss_ref, inv_ref)


def _rglru(x, gains, big_w, conv_w, conv_b, gate_b, lam, h0, hist0, layer, row_tile, segments):
    m, d = x.shape
    d_rnn = big_w[-1].shape[-2]
    nc = d_rnn // CH_TILE
    nb = CH_TILE // RG_BLOCK
    hw = CONV_A - 1
    n_seq = 1 if segments is None else len(segments)
    grid = (m // row_tile, nc)
    tile_f32 = pltpu.VMEM((row_tile, CH_TILE), F32)
    norm_scratch = [pltpu.VMEM((row_tile, LANES), F32)] * 2
    if segments is None:
        kernel = _rglru_main_kernel
        scratch = [
            pltpu.VMEM((row_tile, d), BF16),
            pltpu.VMEM((nc, SUBLANES, CH_TILE), F32),
            tile_f32,
            pltpu.VMEM((row_tile + SUBLANES, CH_TILE), F32),
            tile_f32,
            pltpu.VMEM((row_tile, 2 * CH_TILE), F32),
            pltpu.VMEM((row_tile, CH_TILE), BF16),
        ] + norm_scratch
    else:
        kernel = functools.partial(_rglru_small_kernel, tuple(segments))
        scratch = [
            pltpu.VMEM((row_tile, d), BF16),
            pltpu.VMEM((row_tile + SUBLANES, CH_TILE), F32),
            tile_f32, tile_f32, tile_f32, tile_f32,
        ] + norm_scratch
    n_gate_blocks = d_rnn // RG_BLOCK
    flat_specs = [
        pl.BlockSpec((d, CH_TILE), lambda i, c: (0, c)),
        pl.BlockSpec((d, CH_TILE), lambda i, c: (0, c)),
        pl.BlockSpec((nb, RG_BLOCK, 2 * RG_BLOCK), lambda i, c: (c, 0, 0)),
        pl.BlockSpec((CH_TILE, d), lambda i, c: (c, 0)),
    ]
    flat_shapes = [
        jax.ShapeDtypeStruct((d, d_rnn), BF16),
        jax.ShapeDtypeStruct((d, d_rnn), BF16),
        jax.ShapeDtypeStruct((n_gate_blocks, RG_BLOCK, 2 * RG_BLOCK), BF16),
        jax.ShapeDtypeStruct((d_rnn, d), BF16),
    ]
    if segments is None:
        w_specs, w_args = flat_specs, tuple(big_w)
    else:
        w_in, gate_w, w_out = big_w
        w_specs = [
            pl.BlockSpec((None, d, CH_TILE), lambda i, c: (layer, 0, c)),
            pl.BlockSpec((None, d, CH_TILE), lambda i, c: (layer, 0, nc + c)),
            pl.BlockSpec((None, nb, RG_BLOCK, 2 * RG_BLOCK), lambda i, c: (layer, c, 0, 0)),
            pl.BlockSpec((None, CH_TILE, d), lambda i, c: (layer, c, 0)),
        ]
        w_args = (w_in, w_in, gate_w, w_out)
    out = pl.pallas_call(
        kernel,
        grid=grid,
        in_specs=[
            pl.BlockSpec((row_tile, d), lambda i, c: (i, 0)),
            pl.BlockSpec((2, d), lambda i, c: (0, 0)),
        ] + w_specs + [
            pl.BlockSpec((None, CONV_A, CH_TILE), lambda i, c: (layer, 0, c)),
            pl.BlockSpec((None, 1, CH_TILE), lambda i, c: (layer, 0, c)),
            pl.BlockSpec((None, nb, 1, 2 * RG_BLOCK), lambda i, c: (layer, c, 0, 0)),
            pl.BlockSpec((None, 1, CH_TILE), lambda i, c: (layer, 0, c)),
            pl.BlockSpec((n_seq, CH_TILE), lambda i, c: (0, c)),
            pl.BlockSpec((n_seq, hw, CH_TILE), lambda i, c: (0, 0, c)),
        ],
        out_specs=[
            pl.BlockSpec((row_tile, d), lambda i, c: (i, 0)),
            pl.BlockSpec((None, n_seq, CH_TILE), lambda i, c: (i, 0, c)),
            pl.BlockSpec((None, n_seq, hw, CH_TILE), lambda i, c: (i, 0, 0, c)),
        ] + ([] if segments is None else flat_specs),
        out_shape=[
            jax.ShapeDtypeStruct((m, d), F32),
            jax.ShapeDtypeStruct((grid[0], n_seq, d_rnn), F32),
            jax.ShapeDtypeStruct((grid[0], n_seq, hw, d_rnn), F32),
        ] + ([] if segments is None else flat_shapes),
        scratch_shapes=scratch,
        compiler_params=_compiler_params(),
        name="rglru",
    )(x, gains, *w_args, conv_w, conv_b, gate_b, lam, h0, hist0)
    return tuple(out[:3]) + ((tuple(out[3:]),) if segments is not None else ())


def _segment_conv(stage_ref, seq, hist, w_ref, width):
    n = seq.shape[0]
    hw = width - 1
    stage_ref[SUBLANES - hw:SUBLANES, :] = hist
    stage_ref[SUBLANES:SUBLANES + n, :] = seq
    return (_conv_taps(stage_ref, w_ref, 0, n, width),
            stage_ref[SUBLANES + n - hw:SUBLANES + n, :])


def _rglru_small_kernel(segments,
                        x_ref, g_ref, wgate_ref, wu_ref, gw_ref, wo_ref, cw_ref, cb_ref, gb_ref,
                        lam_ref, h0_ref, hist0_ref,
                        o_ref, hlast_ref, histlast_ref, wgate_copy, wu_copy, gw_copy, wo_copy,
                        xn_ref, stage_ref, u_ref, a_ref, b_ref, h_ref, ss_ref, inv_ref):
    c = pl.program_id(1)

    @pl.when(c == 0)
    def _():
        _prenorm(x_ref, g_ref, xn_ref, ss_ref, inv_ref)
        o_ref[...] = jnp.zeros_like(o_ref)

    xn = xn_ref[...]
    gate = _dot(xn, _weight(wgate_ref, wgate_copy, ...))
    u_pre = _dot(xn, _weight(wu_ref, wu_copy, ...))

    for s, (r0, n) in enumerate(segments):
        conv, new_hist = _segment_conv(stage_ref, u_pre[r0:r0 + n, :], hist0_ref[s], cw_ref, CONV_A)
        u_ref[r0:r0 + n, :] = conv + cb_ref[...]
        histlast_ref[s] = new_hist

    log_sig_lam = _neg_softplus_neg(lam_ref[...])
    for nb in range(CH_TILE // RG_BLOCK):
        cols = slice(nb * RG_BLOCK, (nb + 1) * RG_BLOCK)
        ub = u_ref[:, cols]
        gm = _dot(ub.astype(BF16), _weight(gw_ref, gw_copy, nb)) + gb_ref[nb]
        a, b = _rglru_coeffs(gm[:, :RG_BLOCK], gm[:, RG_BLOCK:], ub, log_sig_lam[:, cols])
        a_ref[:, cols] = a
        b_ref[:, cols] = b

    for s, (r0, n) in enumerate(segments):
        def step(t, h):
            h = a_ref[pl.ds(t, 1), :] * h + b_ref[pl.ds(t, 1), :]
            h_ref[pl.ds(t, 1), :] = h
            return h
        hlast_ref[s:s + 1, :] = lax.fori_loop(r0, r0 + n, step, h0_ref[s:s + 1, :])

    y = (jax.nn.gelu(gate) * h_ref[...]).astype(BF16)
    o_ref[...] += _dot(y, _weight(wo_ref, wo_copy, ...))

    @pl.when(c == pl.num_programs(1) - 1)
    def _():
        _postnorm_residual(x_ref, g_ref, o_ref, 1.0, ss_ref, inv_ref)


def _shortconv_small_kernel(segments,
                            x_ref, g_ref, wb_ref, wc_ref, wv_ref, wo_ref, cw_ref, hist0_ref,
                            o_ref, histlast_ref, wb_copy, wc_copy, wv_copy, wo_copy,
                            xn_ref, stage_ref, y_ref, ss_ref, inv_ref):
    c = pl.program_id(1)

    @pl.when(c == 0)
    def _():
        _prenorm(x_ref, g_ref, xn_ref, ss_ref, inv_ref)
        o_ref[...] = jnp.zeros_like(o_ref)

    xn = xn_ref[...]
    gb = _dot(xn, _weight(wb_ref, wb_copy, ...))
    cv = _dot(xn, _weight(wc_ref, wc_copy, ...)) * _dot(xn, _weight(wv_ref, wv_copy, ...))

    for s, (r0, n) in enumerate(segments):
        z, new_hist = _segment_conv(stage_ref, cv[r0:r0 + n, :], hist0_ref[s], cw_ref, CONV_B)
        y_ref[r0:r0 + n, :] = (gb[r0:r0 + n, :] * z).astype(BF16)
        histlast_ref[s] = new_hist

    o_ref[...] += _dot(y_ref[...], _weight(wo_ref, wo_copy, ...))

    @pl.when(c == pl.num_programs(1) - 1)
    def _():
        _postnorm_residual(x_ref, g_ref, o_ref, 1.0, ss_ref, inv_ref)


def _shortconv_main_kernel(x_ref, g_ref, wb_ref, wc_ref, wv_ref, wo_ref, cw_ref, hist0_ref,
                           o_ref, histlast_ref,
                           xn_ref, state_ref, b_ref, c_ref, stage_ref, y_ref, ss_ref, inv_ref):
    i = pl.program_id(0)
    c = pl.program_id(1)
    n_rows = x_ref.shape[0]
    hw = CONV_B - 1

    @pl.when(c == 0)
    def _():
        _prenorm(x_ref, g_ref, xn_ref, ss_ref, inv_ref)
        o_ref[...] = jnp.zeros_like(o_ref)

    @pl.when(i == 0)
    def _():
        state_ref[c, 0:hw, :] = hist0_ref[0]

    xn = xn_ref[...]
    n_halves = CH_TILE // MXU_COLS

    def half_cols(half):
        return slice(half * MXU_COLS, (half + 1) * MXU_COLS)

    def project(half):
        cols = half_cols(half)
        b_ref[:, cols] = _dot(xn, wb_ref[:, cols])
        c_ref[:, cols] = _dot(xn, wc_ref[:, cols])
        stage_ref[SUBLANES:SUBLANES + n_rows, cols] = _dot(xn, wv_ref[:, cols])

    def conv_and_gate(half):
        cols = half_cols(half)
        for r0 in range(0, n_rows, CONV_ROWS):
            rows = slice(SUBLANES + r0, SUBLANES + r0 + CONV_ROWS)
            stage_ref[rows, cols] = stage_ref[rows, cols] * c_ref[r0:r0 + CONV_ROWS, cols]
        stage_ref[SUBLANES - hw:SUBLANES, cols] = state_ref[c, 0:hw, cols]
        new_hist = stage_ref[SUBLANES + n_rows - hw:SUBLANES + n_rows, cols]
        state_ref[c, 0:hw, cols] = new_hist
        histlast_ref[0, :, cols] = new_hist
        for r0 in range(0, n_rows, CONV_ROWS):
            z = _conv_taps(stage_ref, cw_ref, r0, CONV_ROWS, CONV_B, cols)
            y_ref[r0:r0 + CONV_ROWS, cols] = (b_ref[r0:r0 + CONV_ROWS, cols] * z).astype(BF16)

    def project_out(half):
        o_ref[...] += _dot(y_ref[:, half_cols(half)], wo_ref[half_cols(half), :])

    project(0)
    for half in range(n_halves):
        if half + 1 < n_halves:
            project(half + 1)
        conv_and_gate(half)
        project_out(half)

    @pl.when(c == pl.num_programs(1) - 1)
    def _():
        _postnorm_residual(x_ref, g_ref, o_ref, 1.0, ss_ref, inv_ref)


def _shortconv(x, gains, big_w, conv_w, hist0, layer, row_tile, segments):
    m, d = x.shape
    d_conv = big_w[-1].shape[-2]
    nc = d_conv // CH_TILE
    hw = CONV_B - 1
    n_seq = 1 if segments is None else len(segments)
    grid = (m // row_tile, nc)
    tile_f32 = pltpu.VMEM((row_tile, CH_TILE), F32)
    norm_scratch = [pltpu.VMEM((row_tile, LANES), F32)] * 2
    if segments is None:
        kernel = _shortconv_main_kernel
        scratch = [
            pltpu.VMEM((row_tile, d), BF16),
            pltpu.VMEM((nc, SUBLANES, CH_TILE), F32),
            tile_f32, tile_f32,
            pltpu.VMEM((row_tile + SUBLANES, CH_TILE), F32),
            pltpu.VMEM((row_tile, CH_TILE), BF16),
        ] + norm_scratch
    else:
        kernel = functools.partial(_shortconv_small_kernel, tuple(segments))
        scratch = [
            pltpu.VMEM((row_tile, d), BF16),
            pltpu.VMEM((row_tile + SUBLANES, CH_TILE), F32),
            pltpu.VMEM((row_tile, CH_TILE), BF16),
        ] + norm_scratch
    flat_specs = [pl.BlockSpec((d, CH_TILE), lambda i, c: (0, c))] * 3 + [
        pl.BlockSpec((CH_TILE, d), lambda i, c: (c, 0))]
    flat_shapes = [jax.ShapeDtypeStruct((d, d_conv), BF16)] * 3 + [
        jax.ShapeDtypeStruct((d_conv, d), BF16)]
    if segments is None:
        w_specs, w_args = flat_specs, tuple(big_w)
    else:
        w_in, w_out = big_w
        w_specs = [
            pl.BlockSpec((None, d, CH_TILE), lambda i, c: (layer, 0, c)),
            pl.BlockSpec((None, d, CH_TILE), lambda i, c: (layer, 0, nc + c)),
            pl.BlockSpec((None, d, CH_TILE), lambda i, c: (layer, 0, 2 * nc + c)),
            pl.BlockSpec((None, CH_TILE, d), lambda i, c: (layer, c, 0)),
        ]
        w_args = (w_in, w_in, w_in, w_out)
    out = pl.pallas_call(
        kernel,
        grid=grid,
        in_specs=[
            pl.BlockSpec((row_tile, d), lambda i, c: (i, 0)),
            pl.BlockSpec((2, d), lambda i, c: (0, 0)),
        ] + w_specs + [
            pl.BlockSpec((None, CONV_B, CH_TILE), lambda i, c: (layer, 0, c)),
            pl.BlockSpec((n_seq, hw, CH_TILE), lambda i, c: (0, 0, c)),
        ],
        out_specs=[
            pl.BlockSpec((row_tile, d), lambda i, c: (i, 0)),
            pl.BlockSpec((None, n_seq, hw, CH_TILE), lambda i, c: (i, 0, 0, c)),
        ] + ([] if segments is None else flat_specs),
        out_shape=[
            jax.ShapeDtypeStruct((m, d), F32),
            jax.ShapeDtypeStruct((grid[0], n_seq, hw, d_conv), F32),
        ] + ([] if segments is None else flat_shapes),
        scratch_shapes=scratch,
        compiler_params=_compiler_params(),
        name="shortconv",
    )(x, gains, *w_args, conv_w, hist0)
    return tuple(out[:2]) + ((tuple(out[2:]),) if segments is not None else ())


def kernel(x_prompt, x_sample, state_a_h, cache_a_conv, cache_b_conv, meta_tokens, norm_g, ffn1_wg, ffn1_wu, ffn1_wd, ffn2_wg, ffn2_wu, ffn2_wd, a_w_in, a_conv_w, a_conv_b, a_gate_w, a_gate_b, a_lambda, a_w_out, b_w_in, b_conv_w, b_w_out):
    depth = norm_g.shape[0]
    n_prompt, seq, d = x_prompt.shape
    n_dec, dec_seq, _ = x_sample.shape
    assert n_prompt == 1, "prompt rows are treated as one causal sequence"
    assert seq % ROW_TILE == 0 and seq % FFN_ROW_TILE == 0

    x_main = x_prompt.reshape(seq, d)
    x_small = jnp.concatenate(
        [meta_tokens.astype(F32), x_sample.reshape(n_dec * dec_seq, d)], axis=0)
    m_small = x_small.shape[0]
    small_segments = [(0, N_META)] + [(N_META + b * dec_seq, dec_seq) for b in range(n_dec)]

    a_conv_b3 = a_conv_b[:, None, :]
    a_lambda3 = a_lambda[:, None, :]
    a_gate_b4 = a_gate_b[:, :, None, :]

    p_h, p_ha, p_hb, s_h, s_ha, s_hb = [], [], [], [], [], []
    for layer in range(depth):
        g = norm_g[layer]
        j = layer // 2

        x_small, *w_bf16 = _ffn(x_small, g[0:2], ffn1_wg, ffn1_wu, ffn1_wd, m_small, layer)
        x_main = _ffn(x_main, g[0:2], *w_bf16, FFN_ROW_TILE)

        if layer % 2 == 0:
            h0 = jnp.concatenate([jnp.zeros((1, state_a_h.shape[-1]), F32), state_a_h[j]], axis=0)
            hist0 = jnp.concatenate(
                [jnp.zeros((1,) + cache_a_conv.shape[2:], F32), cache_a_conv[j]], axis=0)
            small_args = (a_conv_w, a_conv_b3, a_gate_b4, a_lambda3)
            x_small, h_s, hist_s, w_bf16 = _rglru(
                x_small, g[2:4], (a_w_in, a_gate_w, a_w_out), *small_args, h0, hist0,
                j, m_small, small_segments)
            h_s, hist_s = h_s[0], hist_s[0]
            x_main, h_m, hist_m = _rglru(x_main, g[2:4], w_bf16, *small_args,
                                         h_s[0:1], hist_s[0:1], j, ROW_TILE, None)
            p_h.append(h_m[-1])
            p_ha.append(hist_m[-1])
            s_h.append(h_s[1:])
            s_ha.append(hist_s[1:])
        else:
            hist0 = jnp.concatenate(
                [jnp.zeros((1,) + cache_b_conv.shape[2:], F32), cache_b_conv[j]], axis=0)
            x_small, hist_s, w_bf16 = _shortconv(x_small, g[2:4], (b_w_in, b_w_out), b_conv_w,
                                                 hist0, j, m_small, small_segments)
            hist_s = hist_s[0]
            x_main, hist_m = _shortconv(x_main, g[2:4], w_bf16, b_conv_w, hist_s[0:1],
                                        j, ROW_TILE, None)
            p_hb.append(hist_m[-1])
            s_hb.append(hist_s[1:])

        x_small, *w_bf16 = _ffn(x_small, g[4:6], ffn2_wg, ffn2_wu, ffn2_wd, m_small, layer)
        x_main = _ffn(x_main, g[4:6], *w_bf16, FFN_ROW_TILE)

    y_prompt = x_main.reshape(1, seq, d)
    y_sample = x_small[N_META:].reshape(n_dec, dec_seq, d)
    return (y_prompt, y_sample,
            jnp.stack(p_h), jnp.stack(p_ha), jnp.stack(p_hb),
            jnp.stack(s_h), jnp.stack(s_ha), jnp.stack(s_hb))
```

```python
import functools

import jax
import jax.numpy as jnp
from jax import lax
from jax.experimental import pallas as pl
from jax.experimental.pallas import tpu as pltpu

F32 = jnp.float32
BF16 = jnp.bfloat16

EPS = 1e-6
RG_C = 8.0
RG_BLOCK = 128
N_META = 16
CONV_A = 4
CONV_B = 3

LANES = 128
SUBLANES = 8
MXU_COLS = 256
VMEM_LIMIT_BYTES = 60 * 1024 * 1024

ROW_TILE = 512
FFN_ROW_TILE = 1024
FF_TILE = 512
CH_TILE = 512
NORM_ROWS = 16
CONV_ROWS = 64
LOOP_UNROLL = 8


def _dot(a, b):
    return jnp.dot(a, b, preferred_element_type=F32)


def _row_chunks(n_rows, rows_per_chunk, body, unroll=LOOP_UNROLL):
    def step(k, _):
        body(pl.multiple_of(k * rows_per_chunk, rows_per_chunk))
        return 0
    lax.fori_loop(0, n_rows // rows_per_chunk, step, 0, unroll=unroll)


def _lane_tile(v, d):
    return jnp.concatenate([v] * (d // LANES), axis=-1)


def _inv_rms_to(src_ref, scale, ss_ref, inv_ref):
    n_rows, d = src_ref.shape

    def partial(r0):
        x = src_ref[pl.ds(r0, SUBLANES), :]
        x2 = x * x
        acc = x2[:, 0:LANES]
        for q in range(1, d // LANES):
            acc = acc + x2[:, q * LANES:(q + 1) * LANES]
        ss_ref[pl.ds(r0, SUBLANES), :] = acc
    _row_chunks(n_rows, SUBLANES, partial)
    tot = jnp.sum(ss_ref[...], axis=-1, keepdims=True)
    inv = scale * lax.rsqrt(tot * (1.0 / d) + EPS)
    inv_ref[...] = jnp.broadcast_to(inv, inv_ref.shape)


def _prenorm(x_ref, g_ref, xn_ref, ss_ref, inv_ref):
    n_rows, d = x_ref.shape
    _inv_rms_to(x_ref, 1.0, ss_ref, inv_ref)
    gain = jnp.broadcast_to(g_ref[0:1, :], (SUBLANES, d))
    gain = jnp.concatenate([gain] * (NORM_ROWS // SUBLANES), axis=0)

    def apply(r0):
        rows = pl.ds(r0, NORM_ROWS)
        inv = _lane_tile(inv_ref[rows, :], d)
        xn_ref[rows, :] = ((x_ref[rows, :] * inv) * gain).astype(BF16)
    _row_chunks(n_rows, NORM_ROWS, apply)


def _postnorm_residual(x_ref, g_ref, o_ref, scale, ss_ref, inv_ref):
    n_rows, d = x_ref.shape
    _inv_rms_to(o_ref, scale, ss_ref, inv_ref)
    gain = jnp.broadcast_to(g_ref[1:2, :], (SUBLANES, d))

    n_slabs = n_rows // SUBLANES

    def apply(k, o_cur):
        rows = pl.ds(pl.multiple_of(k * SUBLANES, SUBLANES), SUBLANES)
        nxt = pl.multiple_of(jnp.minimum(k + 1, n_slabs - 1) * SUBLANES, SUBLANES)
        o_next = o_ref[pl.ds(nxt, SUBLANES), :]
        inv = _lane_tile(inv_ref[rows, :], d)
        o_ref[rows, :] = x_ref[rows, :] + (o_cur * inv) * gain
        return o_next
    lax.fori_loop(0, n_slabs, apply, o_ref[0:SUBLANES, :], unroll=LOOP_UNROLL)


def _compiler_params():
    return pltpu.CompilerParams(
        dimension_semantics=("arbitrary", "arbitrary"),
        vmem_limit_bytes=VMEM_LIMIT_BYTES)


def _weight(w_ref, copy_ref, idx):
    w = w_ref[idx]
    if copy_ref is None:
        return w
    w = w.astype(BF16)
    copy_ref[idx] = w
    return w


def _ffn_kernel(emit_bf16, *refs):
    x_ref, g_ref, wg_ref, wu_ref, wd_ref, o_ref = refs[:6]
    wg_copy, wu_copy, wd_copy = refs[6:9] if emit_bf16 else (None, None, None)
    xn_ref, gate_ref, up_ref, h_ref, ss_ref, inv_ref = refs[-6:]
    j = pl.program_id(1)

    @pl.when(j == 0)
    def _():
        _prenorm(x_ref, g_ref, xn_ref, ss_ref, inv_ref)
        o_ref[...] = jnp.zeros_like(o_ref)

    xn = xn_ref[...]
    n_halves = FF_TILE // MXU_COLS

    def half_cols(half):
        return slice(half * MXU_COLS, (half + 1) * MXU_COLS)

    def project(half):
        cols = half_cols(half)
        gate_ref[:, cols] = _dot(xn, _weight(wg_ref, wg_copy, (slice(None), cols)))
        up_ref[:, cols] = _dot(xn, _weight(wu_ref, wu_copy, (slice(None), cols)))

    def swiglu(half):
        cols = half_cols(half)
        g = gate_ref[:, cols]
        h_ref[:, cols] = (g * jax.nn.sigmoid(g) * up_ref[:, cols]).astype(BF16)

    def project_down(half):
        cols = half_cols(half)
        o_ref[...] += _dot(h_ref[:, cols], _weight(wd_ref, wd_copy, (cols, slice(None))))

    project(0)
    for half in range(n_halves):
        if half + 1 < n_halves:
            project(half + 1)
        swiglu(half)
        project_down(half)

    @pl.when(j == pl.num_programs(1) - 1)
    def _():
        _postnorm_residual(x_ref, g_ref, o_ref, 0.5, ss_ref, inv_ref)


def _ffn(x, gains, wg, wu, wd, row_tile, layer=None):
    m, d = x.shape
    d_ff = wg.shape[-1]
    grid = (m // row_tile, d_ff // FF_TILE)
    emit = layer is not None
    stack_specs = [
        pl.BlockSpec((None, d, FF_TILE), lambda i, j: (layer, 0, j)),
        pl.BlockSpec((None, d, FF_TILE), lambda i, j: (layer, 0, j)),
        pl.BlockSpec((None, FF_TILE, d), lambda i, j: (layer, j, 0)),
    ]
    flat_specs = [
        pl.BlockSpec((d, FF_TILE), lambda i, j: (0, j)),
        pl.BlockSpec((d, FF_TILE), lambda i, j: (0, j)),
        pl.BlockSpec((FF_TILE, d), lambda i, j: (j, 0)),
    ]
    flat_shapes = [jax.ShapeDtypeStruct((d, d_ff), BF16), jax.ShapeDtypeStruct((d, d_ff), BF16),
                   jax.ShapeDtypeStruct((d_ff, d), BF16)]
    x_spec = pl.BlockSpec((row_tile, d), lambda i, j: (i, 0))
    out = pl.pallas_call(
        functools.partial(_ffn_kernel, emit),
        grid=grid,
        in_specs=[x_spec, pl.BlockSpec((2, d), lambda i, j: (0, 0))]
        + (stack_specs if emit else flat_specs),
        out_specs=[x_spec] + (flat_specs if emit else []),
        out_shape=[jax.ShapeDtypeStruct((m, d), F32)] + (flat_shapes if emit else []),
        scratch_shapes=[
            pltpu.VMEM((row_tile, d), BF16),
            pltpu.VMEM((row_tile, FF_TILE), F32),
            pltpu.VMEM((row_tile, FF_TILE), F32),
            pltpu.VMEM((row_tile, FF_TILE), BF16),
            pltpu.VMEM((row_tile, LANES), F32),
            pltpu.VMEM((row_tile, LANES), F32),
        ],
        compiler_params=_compiler_params(),
        name="ffn",
    )(x, gains, wg, wu, wd)
    return out if emit else out[0]


def _neg_softplus_neg(lam):
    return -(jnp.maximum(-lam, 0.0) + jnp.log1p(jnp.exp(-jnp.abs(lam))))


def _rglru_coeffs(r_pre, i_pre, u, log_sig_lam):
    r = jax.nn.sigmoid(r_pre)
    ig = jax.nn.sigmoid(i_pre)
    log_a = RG_C * r * log_sig_lam
    a = jnp.exp(log_a)
    one_minus_a2 = -jnp.tanh(log_a) * (a * a + 1.0)
    return a, jnp.sqrt(one_minus_a2) * (ig * u)


def _conv_taps(stage_ref, w_ref, r0, rows, width, cols=slice(None)):
    hw = width - 1
    out = stage_ref[SUBLANES + r0:SUBLANES + r0 + rows, cols] * w_ref[hw:hw + 1, cols]
    for q in range(hw):
        lo = SUBLANES - hw + q + r0
        out = out + stage_ref[lo:lo + rows, cols] * w_ref[q:q + 1, cols]
    return out


RGLRU_OUT_LAG = 2


def _rglru_main_kernel(n_tiles, n_ch,
                       x_ref, xres_ref, g_ref, wgate_ref, wu_ref, gw_ref, wo_ref, cw_ref, cb_ref,
                       gb_ref, lam_ref, h0_ref, hist0_ref,
                       o_ref, hlast_ref, histlast_ref,
                       xn_ref, state_ref, gate_in_ref, u_in_ref, gate_ref, stage_ref, u_ref,
                       gm_ref, y_ref, y_out_ref, ss_ref, inv_ref):
    n = pl.program_id(0)
    c = (n - 1) % n_ch
    c_out = (n - RGLRU_OUT_LAG) % n_ch
    n_rows = x_ref.shape[0]
    hw = CONV_A - 1
    n_blocks = CH_TILE // RG_BLOCK
    block_cols = [slice(nb * RG_BLOCK, (nb + 1) * RG_BLOCK) for nb in range(n_blocks)]

    @pl.when(n == 0)
    def _():
        gate_ref[...] = jnp.zeros_like(gate_ref)
        stage_ref[...] = jnp.zeros_like(stage_ref)
        y_out_ref[...] = jnp.zeros_like(y_out_ref)

    @pl.when((n % n_ch == 0) & (n < n_tiles * n_ch))
    def _():
        _prenorm(x_ref, g_ref, xn_ref, ss_ref, inv_ref)

    @pl.when((n == 0) | (c_out == 0))
    def _():
        o_ref[...] = jnp.zeros_like(o_ref)

    @pl.when(n <= n_ch)
    def _():
        state_ref[c, 0:1, :] = h0_ref[0:1, :]
        state_ref[c, 1:1 + hw, :] = hist0_ref[0]


    xn = xn_ref[...]
    log_sig_lam = _neg_softplus_neg(lam_ref[...])
    sub = lax.broadcasted_iota(jnp.int32, (SUBLANES, RG_BLOCK), 0)
    n_halves = CH_TILE // MXU_COLS
    blocks_per_half = n_blocks // n_halves

    def half_cols(half):
        return slice(half * MXU_COLS, (half + 1) * MXU_COLS)

    def project_gate(half):
        gate_in_ref[:, half_cols(half)] = _dot(xn, wgate_ref[:, half_cols(half)])

    def project_u(half):
        u_in_ref[:, half_cols(half)] = _dot(xn, wu_ref[:, half_cols(half)])

    def conv_and_gate_matmuls(half):
        cols = half_cols(half)
        stage_ref[SUBLANES - hw:SUBLANES, cols] = state_ref[c, 1:1 + hw, cols]
        new_hist = stage_ref[SUBLANES + n_rows - hw:SUBLANES + n_rows, cols]
        state_ref[c, 1:1 + hw, cols] = new_hist
        histlast_ref[0, :, cols] = new_hist
        for r0 in range(0, n_rows, CONV_ROWS):
            u_ref[r0:r0 + CONV_ROWS, cols] = (
                _conv_taps(stage_ref, cw_ref, r0, CONV_ROWS, CONV_A, cols) + cb_ref[:, cols])
        for nb in range(half * blocks_per_half, (half + 1) * blocks_per_half):
            ub = u_ref[:, block_cols[nb]].astype(BF16)
            gm_ref[:, 2 * nb * RG_BLOCK:2 * (nb + 1) * RG_BLOCK] = _dot(ub, gw_ref[nb])

    def slab_scan(a, b, h_prev):
        for dist in (1, 2, 4):
            a_up = jnp.where(sub >= dist, pltpu.roll(a, dist, 0), 1.0)
            b_up = jnp.where(sub >= dist, pltpu.roll(b, dist, 0), 0.0)
            b = a * b_up + b
            a = a * a_up
        return b + a * h_prev

    def half_blocks(half):
        return range(half * blocks_per_half, (half + 1) * blocks_per_half)

    def recurrence(half, row_lo, row_hi, h_prev):
        for r0 in range(row_lo, row_hi, NORM_ROWS):
            rows = slice(r0, r0 + NORM_ROWS)
            for nb in half_blocks(half):
                cols = block_cols[nb]
                g0 = 2 * nb * RG_BLOCK
                bias = gb_ref[nb]
                a, b = _rglru_coeffs(
                    gm_ref[rows, g0:g0 + RG_BLOCK] + bias[:, :RG_BLOCK],
                    gm_ref[rows, g0 + RG_BLOCK:g0 + 2 * RG_BLOCK] + bias[:, RG_BLOCK:],
                    u_ref[rows, cols], log_sig_lam[:, cols])
                h_lo = slab_scan(a[:SUBLANES], b[:SUBLANES], h_prev[nb])
                h_hi = slab_scan(a[SUBLANES:], b[SUBLANES:], h_lo[SUBLANES - 1:SUBLANES, :])
                h = jnp.concatenate([h_lo, h_hi], axis=0)
                y_ref[rows, cols] = (jax.nn.gelu(gate_ref[rows, cols]) * h).astype(BF16)
                h_prev[nb] = h_hi[SUBLANES - 1:SUBLANES, :]

    def project_out(half, col_lo, col_hi):
        o_ref[:, col_lo:col_hi] += _dot(y_out_ref[:, half_cols(half)],
                                        wo_ref[half_cols(half), col_lo:col_hi])

    for half in range(n_halves):
        h_prev = {nb: state_ref[c, 0:1, block_cols[nb]] for nb in half_blocks(half)}
        project_gate(half)
        conv_and_gate_matmuls(half)
        project_u(half)
        recurrence(half, 0, n_rows, h_prev)
        project_out(half, 0, o_ref.shape[1])
        for nb in half_blocks(half):
            state_ref[c, 0:1, block_cols[nb]] = h_prev[nb]
            hlast_ref[0:1, block_cols[nb]] = h_prev[nb]

    @pl.when((n >= RGLRU_OUT_LAG) & (c_out == n_ch - 1))
    def _():
        _postnorm_residual(xres_ref, g_ref, o_ref, 1.0, ss_ref, inv_ref)

    @pl.when(n >= 0)
    def _():
        gate_ref[...] = gate_in_ref[...]
        stage_ref[SUBLANES:SUBLANES + n_rows, :] = u_in_ref[...]
        y_out_ref[...] = y_ref[...]


def _rglru_main(x, gains, big_w, conv_w, conv_b, gate_b, lam, h0, hist0, layer, row_tile):
    m, d = x.shape
    d_rnn = big_w[-1].shape[-2]
    n_ch = d_rnn // CH_TILE
    n_tiles = m // row_tile
    nb = CH_TILE // RG_BLOCK
    hw = CONV_A - 1
    tile_f32 = pltpu.VMEM((row_tile, CH_TILE), F32)

    def in_tile(n):
        return jnp.minimum(n // n_ch, n_tiles - 1)

    def out_tile(n):
        return jnp.clip((n - RGLRU_OUT_LAG) // n_ch, 0, n_tiles - 1)

    def out_ch(n):
        return (n - 1) % n_ch

    def state_tile(n):
        return jnp.where(n == 0, n_tiles, (n - 1) // n_ch)

    out = pl.pallas_call(
        functools.partial(_rglru_main_kernel, n_tiles, n_ch),
        grid=(n_tiles * n_ch + RGLRU_OUT_LAG,),
        in_specs=[
            pl.BlockSpec((row_tile, d), lambda n: (in_tile(n), 0)),
            pl.BlockSpec((row_tile, d), lambda n: (out_tile(n), 0)),
            pl.BlockSpec((2, d), lambda n: (0, 0)),
            pl.BlockSpec((d, CH_TILE), lambda n: (0, n % n_ch)),
            pl.BlockSpec((d, CH_TILE), lambda n: (0, n % n_ch)),
            pl.BlockSpec((nb, RG_BLOCK, 2 * RG_BLOCK), lambda n: (out_ch(n), 0, 0)),
            pl.BlockSpec((CH_TILE, d), lambda n: ((n - RGLRU_OUT_LAG) % n_ch, 0)),
            pl.BlockSpec((None, CONV_A, CH_TILE), lambda n: (layer, 0, out_ch(n))),
            pl.BlockSpec((None, 1, CH_TILE), lambda n: (layer, 0, out_ch(n))),
            pl.BlockSpec((None, nb, 1, 2 * RG_BLOCK), lambda n: (layer, out_ch(n), 0, 0)),
            pl.BlockSpec((None, 1, CH_TILE), lambda n: (layer, 0, out_ch(n))),
            pl.BlockSpec((1, CH_TILE), lambda n: (0, out_ch(n))),
            pl.BlockSpec((1, hw, CH_TILE), lambda n: (0, 0, out_ch(n))),
        ],
        out_specs=[
            pl.BlockSpec((row_tile, d), lambda n: (out_tile(n), 0)),
            pl.BlockSpec((None, 1, CH_TILE), lambda n: (state_tile(n), 0, out_ch(n))),
            pl.BlockSpec((None, 1, hw, CH_TILE), lambda n: (state_tile(n), 0, 0, out_ch(n))),
        ],
        out_shape=[
            jax.ShapeDtypeStruct((m, d), F32),
            jax.ShapeDtypeStruct((n_tiles + 1, 1, d_rnn), F32),
            jax.ShapeDtypeStruct((n_tiles + 1, 1, hw, d_rnn), F32),
        ],
        scratch_shapes=[
            pltpu.VMEM((row_tile, d), BF16),
            pltpu.VMEM((n_ch, SUBLANES, CH_TILE), F32),
            tile_f32, tile_f32,
            tile_f32,
            pltpu.VMEM((row_tile + SUBLANES, CH_TILE), F32),
            tile_f32,
            pltpu.VMEM((row_tile, 2 * CH_TILE), F32),
            pltpu.VMEM((row_tile, CH_TILE), BF16),
            pltpu.VMEM((row_tile, CH_TILE), BF16),
            pltpu.VMEM((row_tile, LANES), F32),
            pltpu.VMEM((row_tile, LANES), F32),
        ],
        compiler_params=pltpu.CompilerParams(
            dimension_semantics=("arbitrary",), vmem_limit_bytes=VMEM_LIMIT_BYTES),
        name="rglru",
    )(x, x, gains, *big_w, conv_w, conv_b, gate_b, lam, h0, hist0)
    return out[0], out[1][:n_tiles], out[2][:n_tiles]


def _rglru(x, gains, big_w, conv_w, conv_b, gate_b, lam, h0, hist0, layer, row_tile, segments):
    m, d = x.shape
    d_rnn = big_w[-1].shape[-2]
    nc = d_rnn // CH_TILE
    nb = CH_TILE // RG_BLOCK
    hw = CONV_A - 1
    n_seq = len(segments)
    grid = (m // row_tile, nc)
    tile_f32 = pltpu.VMEM((row_tile, CH_TILE), F32)
    norm_scratch = [pltpu.VMEM((row_tile, LANES), F32)] * 2
    kernel = functools.partial(_rglru_small_kernel, tuple(segments))
    scratch = [
        pltpu.VMEM((row_tile, d), BF16),
        pltpu.VMEM((row_tile + SUBLANES, CH_TILE), F32),
        tile_f32, tile_f32, tile_f32, tile_f32,
    ] + norm_scratch
    n_gate_blocks = d_rnn // RG_BLOCK
    flat_specs = [
        pl.BlockSpec((d, CH_TILE), lambda i, c: (0, c)),
        pl.BlockSpec((d, CH_TILE), lambda i, c: (0, c)),
        pl.BlockSpec((nb, RG_BLOCK, 2 * RG_BLOCK), lambda i, c: (c, 0, 0)),
        pl.BlockSpec((CH_TILE, d), lambda i, c: (c, 0)),
    ]
    flat_shapes = [
        jax.ShapeDtypeStruct((d, d_rnn), BF16),
        jax.ShapeDtypeStruct((d, d_rnn), BF16),
        jax.ShapeDtypeStruct((n_gate_blocks, RG_BLOCK, 2 * RG_BLOCK), BF16),
        jax.ShapeDtypeStruct((d_rnn, d), BF16),
    ]
    if segments is None:
        w_specs, w_args = flat_specs, tuple(big_w)
    else:
        w_in, gate_w, w_out = big_w
        w_specs = [
            pl.BlockSpec((None, d, CH_TILE), lambda i, c: (layer, 0, c)),
            pl.BlockSpec((None, d, CH_TILE), lambda i, c: (layer, 0, nc + c)),
            pl.BlockSpec((None, nb, RG_BLOCK, 2 * RG_BLOCK), lambda i, c: (layer, c, 0, 0)),
            pl.BlockSpec((None, CH_TILE, d), lambda i, c: (layer, c, 0)),
        ]
        w_args = (w_in, w_in, gate_w, w_out)
    out = pl.pallas_call(
        kernel,
        grid=grid,
        in_specs=[
            pl.BlockSpec((row_tile, d), lambda i, c: (i, 0)),
            pl.BlockSpec((2, d), lambda i, c: (0, 0)),
        ] + w_specs + [
            pl.BlockSpec((None, CONV_A, CH_TILE), lambda i, c: (layer, 0, c)),
            pl.BlockSpec((None, 1, CH_TILE), lambda i, c: (layer, 0, c)),
            pl.BlockSpec((None, nb, 1, 2 * RG_BLOCK), lambda i, c: (layer, c, 0, 0)),
            pl.BlockSpec((None, 1, CH_TILE), lambda i, c: (layer, 0, c)),
            pl.BlockSpec((n_seq, CH_TILE), lambda i, c: (0, c)),
            pl.BlockSpec((n_seq, hw, CH_TILE), lambda i, c: (0, 0, c)),
        ],
        out_specs=[
            pl.BlockSpec((row_tile, d), lambda i, c: (i, 0)),
            pl.BlockSpec((None, n_seq, CH_TILE), lambda i, c: (i, 0, c)),
            pl.BlockSpec((None, n_seq, hw, CH_TILE), lambda i, c: (i, 0, 0, c)),
        ] + ([] if segments is None else flat_specs),
        out_shape=[
            jax.ShapeDtypeStruct((m, d), F32),
            jax.ShapeDtypeStruct((grid[0], n_seq, d_rnn), F32),
            jax.ShapeDtypeStruct((grid[0], n_seq, hw, d_rnn), F32),
        ] + ([] if segments is None else flat_shapes),
        scratch_shapes=scratch,
        compiler_params=_compiler_params(),
        name="rglru",
    )(x, gains, *w_args, conv_w, conv_b, gate_b, lam, h0, hist0)
    return tuple(out[:3]) + ((tuple(out[3:]),) if segments is not None else ())


def _segment_conv(stage_ref, seq, hist, w_ref, width):
    n = seq.shape[0]
    hw = width - 1
    stage_ref[SUBLANES - hw:SUBLANES, :] = hist
    stage_ref[SUBLANES:SUBLANES + n, :] = seq
    return (_conv_taps(stage_ref, w_ref, 0, n, width),
            stage_ref[SUBLANES + n - hw:SUBLANES + n, :])


def _rglru_small_kernel(segments,
                        x_ref, g_ref, wgate_ref, wu_ref, gw_ref, wo_ref, cw_ref, cb_ref, gb_ref,
                        lam_ref, h0_ref, hist0_ref,
                        o_ref, hlast_ref, histlast_ref, wgate_copy, wu_copy, gw_copy, wo_copy,
                        xn_ref, stage_ref, u_ref, a_ref, b_ref, h_ref, ss_ref, inv_ref):
    c = pl.program_id(1)

    @pl.when(c == 0)
    def _():
        _prenorm(x_ref, g_ref, xn_ref, ss_ref, inv_ref)
        o_ref[...] = jnp.zeros_like(o_ref)

    xn = xn_ref[...]
    gate = _dot(xn, _weight(wgate_ref, wgate_copy, ...))
    u_pre = _dot(xn, _weight(wu_ref, wu_copy, ...))

    for s, (r0, n) in enumerate(segments):
        conv, new_hist = _segment_conv(stage_ref, u_pre[r0:r0 + n, :], hist0_ref[s], cw_ref, CONV_A)
        u_ref[r0:r0 + n, :] = conv + cb_ref[...]
        histlast_ref[s] = new_hist

    log_sig_lam = _neg_softplus_neg(lam_ref[...])
    for nb in range(CH_TILE // RG_BLOCK):
        cols = slice(nb * RG_BLOCK, (nb + 1) * RG_BLOCK)
        ub = u_ref[:, cols]
        gm = _dot(ub.astype(BF16), _weight(gw_ref, gw_copy, nb)) + gb_ref[nb]
        a, b = _rglru_coeffs(gm[:, :RG_BLOCK], gm[:, RG_BLOCK:], ub, log_sig_lam[:, cols])
        a_ref[:, cols] = a
        b_ref[:, cols] = b

    for s, (r0, n) in enumerate(segments):
        def step(t, h):
            h = a_ref[pl.ds(t, 1), :] * h + b_ref[pl.ds(t, 1), :]
            h_ref[pl.ds(t, 1), :] = h
            return h
        hlast_ref[s:s + 1, :] = lax.fori_loop(r0, r0 + n, step, h0_ref[s:s + 1, :])

    y = (jax.nn.gelu(gate) * h_ref[...]).astype(BF16)
    o_ref[...] += _dot(y, _weight(wo_ref, wo_copy, ...))

    @pl.when(c == pl.num_programs(1) - 1)
    def _():
        _postnorm_residual(x_ref, g_ref, o_ref, 1.0, ss_ref, inv_ref)


def _shortconv_small_kernel(segments,
                            x_ref, g_ref, wb_ref, wc_ref, wv_ref, wo_ref, cw_ref, hist0_ref,
                            o_ref, histlast_ref, wb_copy, wc_copy, wv_copy, wo_copy,
                            xn_ref, stage_ref, y_ref, ss_ref, inv_ref):
    c = pl.program_id(1)

    @pl.when(c == 0)
    def _():
        _prenorm(x_ref, g_ref, xn_ref, ss_ref, inv_ref)
        o_ref[...] = jnp.zeros_like(o_ref)

    xn = xn_ref[...]
    gb = _dot(xn, _weight(wb_ref, wb_copy, ...))
    cv = _dot(xn, _weight(wc_ref, wc_copy, ...)) * _dot(xn, _weight(wv_ref, wv_copy, ...))

    for s, (r0, n) in enumerate(segments):
        z, new_hist = _segment_conv(stage_ref, cv[r0:r0 + n, :], hist0_ref[s], cw_ref, CONV_B)
        y_ref[r0:r0 + n, :] = (gb[r0:r0 + n, :] * z).astype(BF16)
        histlast_ref[s] = new_hist

    o_ref[...] += _dot(y_ref[...], _weight(wo_ref, wo_copy, ...))

    @pl.when(c == pl.num_programs(1) - 1)
    def _():
        _postnorm_residual(x_ref, g_ref, o_ref, 1.0, ss_ref, inv_ref)


def _shortconv_main_kernel(x_ref, g_ref, wb_ref, wc_ref, wv_ref, wo_ref, cw_ref, hist0_ref,
                           o_ref, histlast_ref,
                           xn_ref, state_ref, b_ref, c_ref, stage_ref, y_ref, ss_ref, inv_ref):
    i = pl.program_id(0)
    c = pl.program_id(1)
    n_rows = x_ref.shape[0]
    hw = CONV_B - 1

    @pl.when(c == 0)
    def _():
        _prenorm(x_ref, g_ref, xn_ref, ss_ref, inv_ref)
        o_ref[...] = jnp.zeros_like(o_ref)

    @pl.when(i == 0)
    def _():
        state_ref[c, 0:hw, :] = hist0_ref[0]

    xn = xn_ref[...]
    n_halves = CH_TILE // MXU_COLS

    def half_cols(half):
        return slice(half * MXU_COLS, (half + 1) * MXU_COLS)

    def project(half):
        cols = half_cols(half)
        b_ref[:, cols] = _dot(xn, wb_ref[:, cols])
        c_ref[:, cols] = _dot(xn, wc_ref[:, cols])
        stage_ref[SUBLANES:SUBLANES + n_rows, cols] = _dot(xn, wv_ref[:, cols])

    def conv_and_gate(half):
        cols = half_cols(half)
        for r0 in range(0, n_rows, CONV_ROWS):
            rows = slice(SUBLANES + r0, SUBLANES + r0 + CONV_ROWS)
            stage_ref[rows, cols] = stage_ref[rows, cols] * c_ref[r0:r0 + CONV_ROWS, cols]
        stage_ref[SUBLANES - hw:SUBLANES, cols] = state_ref[c, 0:hw, cols]
        new_hist = stage_ref[SUBLANES + n_rows - hw:SUBLANES + n_rows, cols]
        state_ref[c, 0:hw, cols] = new_hist
        histlast_ref[0, :, cols] = new_hist
        for r0 in range(0, n_rows, CONV_ROWS):
            z = _conv_taps(stage_ref, cw_ref, r0, CONV_ROWS, CONV_B, cols)
            y_ref[r0:r0 + CONV_ROWS, cols] = (b_ref[r0:r0 + CONV_ROWS, cols] * z).astype(BF16)

    def project_out(half):
        o_ref[...] += _dot(y_ref[:, half_cols(half)], wo_ref[half_cols(half), :])

    project(0)
    for half in range(n_halves):
        if half + 1 < n_halves:
            project(half + 1)
        conv_and_gate(half)
        project_out(half)

    @pl.when(c == pl.num_programs(1) - 1)
    def _():
        _postnorm_residual(x_ref, g_ref, o_ref, 1.0, ss_ref, inv_ref)


def _shortconv(x, gains, big_w, conv_w, hist0, layer, row_tile, segments):
    m, d = x.shape
    d_conv = big_w[-1].shape[-2]
    nc = d_conv // CH_TILE
    hw = CONV_B - 1
    n_seq = 1 if segments is None else len(segments)
    grid = (m // row_tile, nc)
    tile_f32 = pltpu.VMEM((row_tile, CH_TILE), F32)
    norm_scratch = [pltpu.VMEM((row_tile, LANES), F32)] * 2
    if segments is None:
        kernel = _shortconv_main_kernel
        scratch = [
            pltpu.VMEM((row_tile, d), BF16),
            pltpu.VMEM((nc, SUBLANES, CH_TILE), F32),
            tile_f32, tile_f32,
            pltpu.VMEM((row_tile + SUBLANES, CH_TILE), F32),
            pltpu.VMEM((row_tile, CH_TILE), BF16),
        ] + norm_scratch
    else:
        kernel = functools.partial(_shortconv_small_kernel, tuple(segments))
        scratch = [
            pltpu.VMEM((row_tile, d), BF16),
            pltpu.VMEM((row_tile + SUBLANES, CH_TILE), F32),
            pltpu.VMEM((row_tile, CH_TILE), BF16),
        ] + norm_scratch
    flat_specs = [pl.BlockSpec((d, CH_TILE), lambda i, c: (0, c))] * 3 + [
        pl.BlockSpec((CH_TILE, d), lambda i, c: (c, 0))]
    flat_shapes = [jax.ShapeDtypeStruct((d, d_conv), BF16)] * 3 + [
        jax.ShapeDtypeStruct((d_conv, d), BF16)]
    if segments is None:
        w_specs, w_args = flat_specs, tuple(big_w)
    else:
        w_in, w_out = big_w
        w_specs = [
            pl.BlockSpec((None, d, CH_TILE), lambda i, c: (layer, 0, c)),
            pl.BlockSpec((None, d, CH_TILE), lambda i, c: (layer, 0, nc + c)),
            pl.BlockSpec((None, d, CH_TILE), lambda i, c: (layer, 0, 2 * nc + c)),
            pl.BlockSpec((None, CH_TILE, d), lambda i, c: (layer, c, 0)),
        ]
        w_args = (w_in, w_in, w_in, w_out)
    out = pl.pallas_call(
        kernel,
        grid=grid,
        in_specs=[
            pl.BlockSpec((row_tile, d), lambda i, c: (i, 0)),
            pl.BlockSpec((2, d), lambda i, c: (0, 0)),
        ] + w_specs + [
            pl.BlockSpec((None, CONV_B, CH_TILE), lambda i, c: (layer, 0, c)),
            pl.BlockSpec((n_seq, hw, CH_TILE), lambda i, c: (0, 0, c)),
        ],
        out_specs=[
            pl.BlockSpec((row_tile, d), lambda i, c: (i, 0)),
            pl.BlockSpec((None, n_seq, hw, CH_TILE), lambda i, c: (i, 0, 0, c)),
        ] + ([] if segments is None else flat_specs),
        out_shape=[
            jax.ShapeDtypeStruct((m, d), F32),
            jax.ShapeDtypeStruct((grid[0], n_seq, hw, d_conv), F32),
        ] + ([] if segments is None else flat_shapes),
        scratch_shapes=scratch,
        compiler_params=_compiler_params(),
        name="shortconv",
    )(x, gains, *w_args, conv_w, hist0)
    return tuple(out[:2]) + ((tuple(out[2:]),) if segments is not None else ())


def kernel(x_prompt, x_sample, state_a_h, cache_a_conv, cache_b_conv, meta_tokens, norm_g, ffn1_wg, ffn1_wu, ffn1_wd, ffn2_wg, ffn2_wu, ffn2_wd, a_w_in, a_conv_w, a_conv_b, a_gate_w, a_gate_b, a_lambda, a_w_out, b_w_in, b_conv_w, b_w_out):
    depth = norm_g.shape[0]
    n_prompt, seq, d = x_prompt.shape
    n_dec, dec_seq, _ = x_sample.shape
    assert n_prompt == 1, "prompt rows are treated as one causal sequence"
    assert seq % ROW_TILE == 0 and seq % FFN_ROW_TILE == 0

    x_main = x_prompt.reshape(seq, d)
    x_small = jnp.concatenate(
        [meta_tokens.astype(F32), x_sample.reshape(n_dec * dec_seq, d)], axis=0)
    m_small = x_small.shape[0]
    small_segments = [(0, N_META)] + [(N_META + b * dec_seq, dec_seq) for b in range(n_dec)]

    a_conv_b3 = a_conv_b[:, None, :]
    a_lambda3 = a_lambda[:, None, :]
    a_gate_b4 = a_gate_b[:, :, None, :]

    p_h, p_ha, p_hb, s_h, s_ha, s_hb = [], [], [], [], [], []
    for layer in range(depth):
        g = norm_g[layer]
        j = layer // 2

        x_small, *w_bf16 = _ffn(x_small, g[0:2], ffn1_wg, ffn1_wu, ffn1_wd, m_small, layer)
        x_main = _ffn(x_main, g[0:2], *w_bf16, FFN_ROW_TILE)

        if layer % 2 == 0:
            h0 = jnp.concatenate([jnp.zeros((1, state_a_h.shape[-1]), F32), state_a_h[j]], axis=0)
            hist0 = jnp.concatenate(
                [jnp.zeros((1,) + cache_a_conv.shape[2:], F32), cache_a_conv[j]], axis=0)
            small_args = (a_conv_w, a_conv_b3, a_gate_b4, a_lambda3)
            x_small, h_s, hist_s, w_bf16 = _rglru(
                x_small, g[2:4], (a_w_in, a_gate_w, a_w_out), *small_args, h0, hist0,
                j, m_small, small_segments)
            h_s, hist_s = h_s[0], hist_s[0]
            x_main, h_m, hist_m = _rglru_main(x_main, g[2:4], w_bf16, *small_args,
                                              h_s[0:1], hist_s[0:1], j, ROW_TILE)
            p_h.append(h_m[-1])
            p_ha.append(hist_m[-1])
            s_h.append(h_s[1:])
            s_ha.append(hist_s[1:])
        else:
            hist0 = jnp.concatenate(
                [jnp.zeros((1,) + cache_b_conv.shape[2:], F32), cache_b_conv[j]], axis=0)
            x_small, hist_s, w_bf16 = _shortconv(x_small, g[2:4], (b_w_in, b_w_out), b_conv_w,
                                                 hist0, j, m_small, small_segments)
            hist_s = hist_s[0]
            x_main, hist_m = _shortconv(x_main, g[2:4], w_bf16, b_conv_w, hist_s[0:1],
                                        j, ROW_TILE, None)
            p_hb.append(hist_m[-1])
            s_hb.append(hist_s[1:])

        x_small, *w_bf16 = _ffn(x_small, g[4:6], ffn2_wg, ffn2_wu, ffn2_wd, m_small, layer)
        x_main = _ffn(x_main, g[4:6], *w_bf16, FFN_ROW_TILE)

    y_prompt = x_main.reshape(1, seq, d)
    y_sample = x_small[N_META:].reshape(n_dec, dec_seq, d)
    return (y_prompt, y_sample,
            jnp.stack(p_h), jnp.stack(p_ha), jnp.stack(p_hb),
            jnp.stack(s_h), jnp.stack(s_ha), jnp.stack(s_hb))
```
